```python
import jax, jax.numpy as jnp
from jax import lax
import numpy as np

D_MODEL = 1024
BATCH = 16
SEQ = 2048
DEPTH = 2

D_FF = 2816
CHUNK = 128
A_GROUPS = 8
A_GROUP_DIM = 128
A_WIDTH = A_GROUPS * A_GROUP_DIM
B_HEAD_DIM = 64
B_WIDTH = D_MODEL
B_HEADS = B_WIDTH // B_HEAD_DIM
DECAY_LORA = 64
AAA_LORA = 64
MV_LORA = 32
GATE_WIDTH = 2 * D_MODEL
SHIFT_WIDTH_FIRST = 3 * B_WIDTH + DECAY_LORA + AAA_LORA
SHIFT_WIDTH_REST = SHIFT_WIDTH_FIRST + MV_LORA
IN_WIDTH_FIRST = GATE_WIDTH + 2 * A_WIDTH + SHIFT_WIDTH_FIRST
IN_WIDTH_REST = GATE_WIDTH + 2 * A_WIDTH + SHIFT_WIDTH_REST

NORM_EPS = 1e-6
LN_EPS = 1e-5
GN_EPS = 64e-5

kernel_name = "hybrid_gmlp_rwkv7_macaron"


def rms_norm(x, g):
    xf = x.astype(jnp.float32)
    xf = xf * lax.rsqrt(jnp.mean(xf * xf, axis=-1, keepdims=True) + NORM_EPS)
    return (xf * g).astype(x.dtype)


def layer_norm(x, g, b):
    xf = x.astype(jnp.float32)
    mu = jnp.mean(xf, axis=-1, keepdims=True)
    var = jnp.mean(jnp.square(xf - mu), axis=-1, keepdims=True)
    return ((xf - mu) * lax.rsqrt(var + LN_EPS) * g + b).astype(x.dtype)


def swiglu_ffn(h, w_gu, w_down):
    gate, up = jnp.split(h @ w_gu, 2, axis=-1)
    return (jax.nn.silu(gate) * up) @ w_down


def token_shift_mix(p, mu):
    prev = jnp.pad(p, ((0, 0), (1, 0), (0, 0)))[:, :-1]
    return p + mu * (prev - p)


def chunked_sgu(u, v, ln_g, ln_b, w_s, b_s):
    u = jax.nn.gelu(u)
    v = layer_norm(jax.nn.gelu(v), ln_g, ln_b)
    bsz, t, _ = v.shape
    vc = v.reshape(bsz, t // CHUNK, CHUNK, A_GROUPS, A_GROUP_DIM)
    causal = jnp.tril(jnp.ones((CHUNK, CHUNK), dtype=bool))
    ws = jnp.where(causal, w_s, 0)
    s = jnp.einsum('gts,bcsgd->bctgd', ws, vc) + b_s.T[:, :, None]
    return u * s.reshape(bsz, t, A_WIDTH)


def rwkv7_time_mix(p, v_first, w0, w2, a0, a2, k_k, k_a, r_k, gn_g, gn_b, v0, v2):
    out_dtype = p.dtype
    p = p.astype(jnp.float32)
    bsz, t, _ = p.shape
    r = p[..., :B_WIDTH]
    k = p[..., B_WIDTH:2 * B_WIDTH]
    v = p[..., 2 * B_WIDTH:3 * B_WIDTH]
    off = 3 * B_WIDTH
    w_down = p[..., off:off + DECAY_LORA]
    off += DECAY_LORA
    a_down = p[..., off:off + AAA_LORA]
    off += AAA_LORA
    log_w = -jax.nn.softplus(-(w0 + jnp.tanh(w_down) @ w2)) - 0.5
    decay = jnp.exp(-jnp.exp(log_w))
    a = jax.nn.sigmoid(a0 + a_down @ a2)
    if v_first is None:
        v_first = v
    else:
        v_res_down = p[..., off:off + MV_LORA]
        v = v + (v_first - v) * jax.nn.sigmoid(v0 + v_res_down @ v2)

    def heads(z):
        return z.reshape(bsz, t, B_HEADS, B_HEAD_DIM)

    kk = heads(k * k_k)
    kk = kk / jnp.maximum(jnp.sqrt(jnp.sum(kk * kk, axis=-1, keepdims=True)), 1e-12)
    k = k * (1 + (a - 1) * k_a)
    r, k, v, decay, a = heads(r), heads(k), heads(v), heads(decay), heads(a)

    def time_major(z):
        return jnp.swapaxes(z, 0, 1)

    def step(state, inp):
        r_t, w_t, k_t, v_t, kk_t, a_t = inp
        sa = jnp.einsum('bhvk,bhk->bhv', state, kk_t)
        state = (state * w_t[:, :, None, :]
                 - sa[..., None] * (kk_t * a_t)[:, :, None, :]
                 + v_t[..., None] * k_t[:, :, None, :])
        return state, jnp.einsum('bhvk,bhk->bhv', state, r_t)

    state0 = jnp.zeros((bsz, B_HEADS, B_HEAD_DIM, B_HEAD_DIM), jnp.float32)
    _, y = lax.scan(step, state0, (time_major(r), time_major(decay), time_major(k),
                                   time_major(v), time_major(kk), time_major(a)))
    y = jnp.swapaxes(y, 0, 1)
    mu = jnp.mean(y, axis=-1, keepdims=True)
    var = jnp.mean(jnp.square(y - mu), axis=-1, keepdims=True)
    y = (y - mu) * lax.rsqrt(var + GN_EPS)
    y = y.reshape(bsz, t, B_WIDTH) * gn_g + gn_b
    bonus = jnp.sum(r * k * r_k, axis=-1, keepdims=True) * v
    y = y + bonus.reshape(bsz, t, B_WIDTH)
    return y.astype(out_dtype), v_first


def setup_inputs(seed: int = 0) -> dict:
    key = jax.random.key(seed)
    ks = iter(jax.random.split(key, 40))
    nrm = lambda shape, s: jax.random.normal(next(ks), shape, jnp.float32) * s
    gain = lambda shape: 1.0 + nrm(shape, 0.02)
    nr = max(DEPTH - 1, 0)
    return {
        "x": nrm((BATCH, SEQ, D_MODEL), 1.0),
        "w_in_first": nrm((D_MODEL, IN_WIDTH_FIRST), D_MODEL ** -0.5),
        "mu_first": jax.random.uniform(next(ks), (SHIFT_WIDTH_FIRST,), jnp.float32),
        "w_in_rest": nrm((nr, D_MODEL, IN_WIDTH_REST), D_MODEL ** -0.5),
        "mu_rest": jax.random.uniform(next(ks), (nr, SHIFT_WIDTH_REST), jnp.float32),
        "rwkv_v0": 1.0 + nrm((nr, B_WIDTH), 0.1),
        "rwkv_v2": nrm((nr, MV_LORA, B_WIDTH), 0.5 * MV_LORA ** -0.5),
        "ffn1_norm": gain((DEPTH, D_MODEL)),
        "ffn1_w_gu": nrm((DEPTH, D_MODEL, 2 * D_FF), D_MODEL ** -0.5),
        "ffn1_w_down": nrm((DEPTH, D_FF, D_MODEL), D_FF ** -0.5),
        "mix_norm": gain((DEPTH, D_MODEL)),
        "sgu_ln_g": gain((DEPTH, A_WIDTH)),
        "sgu_ln_b": nrm((DEPTH, A_WIDTH), 0.02),
        "sgu_w_s": nrm((DEPTH, A_GROUPS, CHUNK, CHUNK), CHUNK ** -0.5),
        "sgu_b_s": 1.0 + nrm((DEPTH, A_GROUPS, CHUNK), 0.02),
        "rwkv_w0": jax.random.uniform(next(ks), (DEPTH, B_WIDTH), jnp.float32, -5.0, 0.0),
        "rwkv_w2": nrm((DEPTH, DECAY_LORA, B_WIDTH), 0.5 * DECAY_LORA ** -0.5),
        "rwkv_a0": nrm((DEPTH, B_WIDTH), 0.1),
        "rwkv_a2": nrm((DEPTH, AAA_LORA, B_WIDTH), 0.5 * AAA_LORA ** -0.5),
        "rwkv_k_k": 0.85 + nrm((DEPTH, B_WIDTH), 0.02),
        "rwkv_k_a": gain((DEPTH, B_WIDTH)),
        "rwkv_r_k": nrm((DEPTH, B_HEADS, B_HEAD_DIM), 0.1),
        "rwkv_gn_g": gain((DEPTH, B_WIDTH)),
        "rwkv_gn_b": nrm((DEPTH, B_WIDTH), 0.02),
        "w_proj_a": nrm((DEPTH, A_WIDTH, D_MODEL), A_WIDTH ** -0.5),
        "w_proj_b": nrm((DEPTH, B_WIDTH, D_MODEL), B_WIDTH ** -0.5),
        "w_out": nrm((DEPTH, D_MODEL, D_MODEL), D_MODEL ** -0.5),
        "ffn2_norm": gain((DEPTH, D_MODEL)),
        "ffn2_w_gu": nrm((DEPTH, D_MODEL, 2 * D_FF), D_MODEL ** -0.5),
        "ffn2_w_down": nrm((DEPTH, D_FF, D_MODEL), D_FF ** -0.5),
        "final_norm": gain((D_MODEL,)),
    }


def reference(x, w_in_first, mu_first, w_in_rest, mu_rest, rwkv_v0, rwkv_v2,
              ffn1_norm, ffn1_w_gu, ffn1_w_down, mix_norm,
              sgu_ln_g, sgu_ln_b, sgu_w_s, sgu_b_s,
              rwkv_w0, rwkv_w2, rwkv_a0, rwkv_a2, rwkv_k_k, rwkv_k_a, rwkv_r_k,
              rwkv_gn_g, rwkv_gn_b, w_proj_a, w_proj_b, w_out,
              ffn2_norm, ffn2_w_gu, ffn2_w_down, final_norm):
    v_first = None
    for layer in range(DEPTH):
        h = rms_norm(x, ffn1_norm[layer])
        x = x + 0.5 * swiglu_ffn(h, ffn1_w_gu[layer], ffn1_w_down[layer])

        h = rms_norm(x, mix_norm[layer])
        if layer == 0:
            p = h @ w_in_first
            mu = mu_first
            v0, v2 = None, None
        else:
            p = h @ w_in_rest[layer - 1]
            mu = mu_rest[layer - 1]
            v0, v2 = rwkv_v0[layer - 1], rwkv_v2[layer - 1]
        gate_a = p[..., :D_MODEL]
        gate_b = p[..., D_MODEL:GATE_WIDTH]
        u_a = p[..., GATE_WIDTH:GATE_WIDTH + A_WIDTH]
        v_a = p[..., GATE_WIDTH + A_WIDTH:GATE_WIDTH + 2 * A_WIDTH]
        p_b = token_shift_mix(p[..., GATE_WIDTH + 2 * A_WIDTH:], mu)

        y_a = chunked_sgu(u_a, v_a, sgu_ln_g[layer], sgu_ln_b[layer],
                          sgu_w_s[layer], sgu_b_s[layer])
        y_b, v_first = rwkv7_time_mix(p_b, v_first, rwkv_w0[layer], rwkv_w2[layer],
                                      rwkv_a0[layer], rwkv_a2[layer], rwkv_k_k[layer],
                                      rwkv_k_a[layer], rwkv_r_k[layer],
                                      rwkv_gn_g[layer], rwkv_gn_b[layer], v0, v2)

        merged = (jax.nn.sigmoid(gate_a) * (y_a @ w_proj_a[layer])
                  + jax.nn.sigmoid(gate_b) * (y_b @ w_proj_b[layer]))
        x = x + merged @ w_out[layer]

        h = rms_norm(x, ffn2_norm[layer])
        x = x + 0.5 * swiglu_ffn(h, ffn2_w_gu[layer], ffn2_w_down[layer])
    return rms_norm(x, final_norm)
```

```python
import functools

import jax
import jax.numpy as jnp
from jax import lax
from jax.experimental import pallas as pl
from jax.experimental.pallas import tpu as pltpu

F32 = jnp.float32
BF16 = jnp.bfloat16

NORM_EPS = 1e-6
LN_EPS = 1e-5
GN_EPS = 64e-5

SGU_CHUNK = 128
SGU_GROUP_DIM = 128
HEAD_DIM = 64

V7X_MXU_DIM = 256
V7X_SUBLANES = 8
V7X_VMEM_LIMIT_BYTES = 60000 * 1024

SCAN_CHUNK = 64
HEADS_PER_TILE = V7X_MXU_DIM // HEAD_DIM
TILE = V7X_MXU_DIM


def _params(*semantics):
    return pltpu.CompilerParams(dimension_semantics=semantics, vmem_limit_bytes=V7X_VMEM_LIMIT_BYTES)


def _const_spec(shape):
    return pl.BlockSpec(shape, lambda *_: (0,) * len(shape), pipeline_mode=pl.Buffered(1))


def _rms(x, g):
    return x * lax.rsqrt(jnp.mean(x * x, axis=-1, keepdims=True) + NORM_EPS) * g


def _dot(a, b):
    return jnp.dot(a, b, preferred_element_type=F32)


def _dot_nt(a, b):
    return lax.dot_general(a, b, (((1,), (1,)), ((), ())), preferred_element_type=F32)


def _dot_tn(a, b):
    return lax.dot_general(a, b, (((0,), (0,)), ((), ())), preferred_element_type=F32)


def _ffn_body(*refs, d_ff, f_chunk, final):
    if final:
        x_ref, g_ref, wgu_ref, wd_ref, fg_ref, o_ref = refs
    else:
        x_ref, g_ref, wgu_ref, wd_ref, o_ref = refs
    x = x_ref[...]
    h = _rms(x, g_ref[...]).astype(BF16)
    acc = jnp.zeros_like(x)
    for c in range(d_ff // f_chunk):
        lo = c * f_chunk
        gate = _dot(h, wgu_ref[:, lo:lo + f_chunk])
        up = _dot(h, wgu_ref[:, d_ff + lo:d_ff + lo + f_chunk])
        act = (gate * jax.nn.sigmoid(gate) * up).astype(BF16)
        acc = acc + _dot(act, wd_ref[lo:lo + f_chunk, :])
    y = x + 0.5 * acc
    if final:
        y = _rms(y, fg_ref[...])
    o_ref[...] = y


def _ffn(x, g, wgu, wd, final_g=None, *, tm=512):
    n, d = x.shape
    d_ff = wd.shape[0]
    final = final_g is not None
    row = pl.BlockSpec((tm, d), lambda i: (i, 0))
    in_specs = [row, _const_spec((1, d)), _const_spec(wgu.shape), _const_spec(wd.shape)]
    args = [x, g.reshape(1, d), wgu, wd]
    if final:
        in_specs.append(_const_spec((1, d)))
        args.append(final_g.reshape(1, d))
    return pl.pallas_call(
        functools.partial(_ffn_body, d_ff=d_ff, f_chunk=V7X_MXU_DIM, final=final),
        grid=(n // tm,),
        in_specs=in_specs,
        out_specs=row,
        out_shape=jax.ShapeDtypeStruct((n, d), F32),
        compiler_params=_params("parallel"),
        name="ffn_final" if final else "ffn",
    )(*args)


def _inproj_body(x_ref, g_ref, w_ref, wl_ref, lng_ref, lnb_ref,
                 ga_ref, gb_ref, gu_ref, vln_ref, prkv_ref, plora_ref):
    d = x_ref.shape[1]
    h = _rms(x_ref[...], g_ref[...]).astype(BF16)
    ga_ref[...] = _dot(h, w_ref[:, 0:d])
    gb_ref[...] = _dot(h, w_ref[:, d:2 * d])
    gu_ref[...] = jax.nn.gelu(_dot(h, w_ref[:, 2 * d:3 * d])).astype(BF16)
    va = jax.nn.gelu(_dot(h, w_ref[:, 3 * d:4 * d]))
    mu = jnp.mean(va, axis=-1, keepdims=True)
    vc = va - mu
    var = jnp.mean(vc * vc, axis=-1, keepdims=True)
    vln_ref[...] = (vc * lax.rsqrt(var + LN_EPS) * lng_ref[...] + lnb_ref[...]).astype(BF16)
    for c in range(3):
        prkv_ref[:, c * d:(c + 1) * d] = _dot(h, w_ref[:, (4 + c) * d:(5 + c) * d])
    plora_ref[...] = _dot(h, wl_ref[...])


def _inproj(x, g, w_main, w_lora, ln_g, ln_b, *, tm=512):
    n, d = x.shape
    lw = w_lora.shape[1]
    row = lambda w: pl.BlockSpec((tm, w), lambda i: (i, 0))
    return pl.pallas_call(
        _inproj_body,
        grid=(n // tm,),
        in_specs=[row(d), _const_spec((1, d)), _const_spec(w_main.shape), _const_spec(w_lora.shape),
                  _const_spec((1, d)), _const_spec((1, d))],
        out_specs=[row(d), row(d), row(d), row(d), row(3 * d), row(lw)],
        out_shape=[jax.ShapeDtypeStruct((n, d), F32), jax.ShapeDtypeStruct((n, d), F32),
                   jax.ShapeDtypeStruct((n, d), BF16), jax.ShapeDtypeStruct((n, d), BF16),
                   jax.ShapeDtypeStruct((n, 3 * d), F32), jax.ShapeDtypeStruct((n, lw), F32)],
        compiler_params=_params("parallel"),
        name="inproj",
    )(x, g.reshape(1, d), w_main, w_lora, ln_g.reshape(1, d), ln_b.reshape(1, d))


def _split_bf16(x, terms):
    parts = []
    for i in range(terms):
        p = x.astype(BF16)
        parts.append(p)
        if i + 1 < terms:
            x = x - p.astype(F32)
    return parts


def _head_sum(x, ones_bd):
    out = []
    for q in range(x.shape[1] // TILE):
        xs = x[:, q * TILE:(q + 1) * TILE]
        out.append(sum(_dot(p, ones_bd) for p in _split_bf16(xs, 2)))
    return jnp.concatenate(out, axis=1)


def _rwkv_body(*refs, has_vfirst, n_sub):
    if has_vfirst:
        (prkv_ref, plora_ref, vf_ref, mu_ref, mul_ref, w0_ref, a0_ref, v0_ref, kk_ref, ka_ref, rk_ref,
         gng_ref, gnb_ref, w2_ref, a2_ref, v2_ref, yb_ref, carry_ref, carryl_ref, s_ref) = refs
    else:
        (prkv_ref, plora_ref, mu_ref, mul_ref, w0_ref, a0_ref, kk_ref, ka_ref, rk_ref,
         gng_ref, gnb_ref, w2_ref, a2_ref, yb_ref, vfo_ref, carry_ref, carryl_ref, s_ref) = refs
    C = SCAN_CHUNK
    d = yb_ref.shape[1]
    n_tiles = d // TILE

    @pl.when(pl.program_id(1) == 0)
    def _():
        carry_ref[...] = jnp.zeros_like(carry_ref)
        carryl_ref[...] = jnp.zeros_like(carryl_ref)
        s_ref[...] = jnp.zeros_like(s_ref)

    def iota(shape, dim):
        return lax.broadcasted_iota(jnp.int32, shape, dim)

    ones_bd = (iota((TILE, TILE), 0) // HEAD_DIM == iota((TILE, TILE), 1) // HEAD_DIM)
    tril_ones = (iota((C, C), 0) >= iota((C, C), 1)).astype(BF16)
    tt = iota((C, TILE), 0)
    cc = iota((C, TILE), 1) % HEAD_DIM
    strict = cc < tt
    incl = cc <= tt
    incl2 = jnp.concatenate([incl, incl], axis=1)
    eye = (cc == tt).astype(F32)
    head_of_lane = iota((C, TILE), 1) // HEAD_DIM
    first_row = iota((V7X_SUBLANES, 1), 0) == 0

    def bstack(m):
        return jnp.concatenate(
            [jnp.where(head_of_lane == h, m, 0.0) for h in range(HEADS_PER_TILE)], axis=0).astype(BF16)

    def level_mask(s):
        return (tt // (2 * s) == cc // (2 * s)) & (tt % (2 * s) >= s) & (cc % (2 * s) < s)

    def shift_mix(ref, cref, mu, rows, cols):
        p = ref[rows, cols]
        rolled = pltpu.roll(p, 1, 0)
        first = jnp.where(first_row, cref[V7X_SUBLANES - 1:V7X_SUBLANES, cols], rolled[0:V7X_SUBLANES])
        prev = jnp.concatenate([first, rolled[V7X_SUBLANES:]], axis=0)
        cref[:, cols] = p[C - V7X_SUBLANES:C]
        return p + mu[:, cols] * (prev - p)

    def chunk(j, carry):
        rows = pl.ds(pl.multiple_of(j * C, C), C)
        ones_b = ones_bd.astype(BF16)
        plm = shift_mix(plora_ref, carryl_ref, mul_ref, rows, slice(None))
        plb = plm.astype(BF16)
        lw = _dot(jnp.tanh(plm).astype(BF16), w2_ref[...])
        la = _dot(plb, a2_ref[...])
        z = -(w0_ref[...] + lw)
        softplus = jnp.maximum(z, 0.0) + jnp.log1p(jnp.exp(-jnp.abs(z)))
        ld = -jnp.exp(-softplus - 0.5)
        a = jax.nn.sigmoid(a0_ref[...] + la)
        r = shift_mix(prkv_ref, carry_ref, mu_ref, rows, slice(0, d))
        k = shift_mix(prkv_ref, carry_ref, mu_ref, rows, slice(d, 2 * d))
        v = shift_mix(prkv_ref, carry_ref, mu_ref, rows, slice(2 * d, 3 * d))
        if has_vfirst:
            lv = _dot(plb, v2_ref[...])
            v = v + (vf_ref[rows, :] - v) * jax.nn.sigmoid(v0_ref[...] + lv)
        else:
            vfo_ref[rows, :] = v
        kk = k * kk_ref[...]
        kk = kk * lax.rsqrt(jnp.maximum(_head_sum(kk * kk, ones_b), 1e-24))
        k = k * (1.0 + (a - 1.0) * ka_ref[...])
        b = kk * a
        cl = sum(_dot(tril_ones, p) for p in _split_bf16(ld, 3))
        cl_last = cl[C - 1:C, :]
        w_inc = jnp.exp(cl)
        w_exc = jnp.exp(cl - ld)
        w_inv = jnp.exp(-cl)
        w_rem = jnp.exp(cl_last - cl)
        w_all = jnp.exp(cl_last)
        rt, kt, bt, kd, bw, kw = r * w_inc, kk * w_exc, b * w_inv, k * w_inv, b * w_rem, k * w_rem

        ys = []
        for q in range(n_tiles):
            sl = slice(q * TILE, (q + 1) * TILE)
            x = jnp.concatenate([kt[:, sl], rt[:, sl]], axis=0).astype(BF16)
            zz = jnp.concatenate([bstack(bt[:, sl]), bstack(kd[:, sl])], axis=0)
            am = _dot_nt(x, zz)
            nq = jnp.where(strict, am[0:C, 0:TILE], 0.0)
            akq = jnp.where(strict, am[0:C, TILE:2 * TILE], 0.0)
            abot = jnp.where(incl2, am[C:2 * C, :], 0.0)
            s0 = s_ref[q]
            g = _dot_nt(x, s0.astype(BF16))
            vq = v[:, sl]
            vstack = bstack(vq)
            rhs = -(g[0:C] + _dot(akq.astype(BF16), vstack))
            p = eye - jnp.where(level_mask(1), nq, 0.0)
            s = 2
            while s < C:
                e = jnp.where(level_mask(s), nq, 0.0)
                pe = _dot(p.astype(BF16), bstack(e))
                p = p - _dot(pe.astype(BF16), bstack(p))
                s *= 2
            u = _dot(p.astype(BF16), bstack(rhs))
            y = g[C:2 * C] + _dot(abot.astype(BF16), jnp.concatenate([bstack(u), vstack], axis=0))
            uv = jnp.concatenate([u, vq], axis=0).astype(BF16)
            bk = jnp.concatenate([bw[:, sl], kw[:, sl]], axis=0).astype(BF16)
            upd = _dot_tn(uv, bk)
            s_ref[q] = s0 * w_all[:, sl] + jnp.where(ones_bd, upd, 0.0)
            ys.append(y)
        y = jnp.concatenate(ys, axis=1)

        mean = _head_sum(y, ones_b) * (1.0 / HEAD_DIM)
        yc = y - mean
        var = _head_sum(yc * yc, ones_b) * (1.0 / HEAD_DIM)
        yn = yc * lax.rsqrt(var + GN_EPS) * gng_ref[...] + gnb_ref[...]
        bonus = _head_sum(r * k * rk_ref[...], ones_b) * v
        yb_ref[rows, :] = (yn + bonus).astype(BF16)
        return carry

    lax.fori_loop(0, n_sub, chunk, 0)


def _rwkv(prkv, plora, vfirst, mu_rkv, mu_lora, w0, a0, v0, k_k, k_a, r_k, gn_g, gn_b, w2p, a2p, v2p,
          *, batch, seq, tc=512):
    n, d3 = prkv.shape
    d = d3 // 3
    lw = plora.shape[1]
    nt = seq // tc
    has_vfirst = vfirst is not None
    row = lambda w: pl.BlockSpec((tc, w), lambda b, t: (b * nt + t, 0))
    vec = lambda a: a.reshape(1, -1)
    in_specs = [row(d3), row(lw)]
    args = [prkv, plora]
    if has_vfirst:
        in_specs.append(row(d))
        args.append(vfirst)
    vecs = [mu_rkv, mu_lora, w0, a0] + ([v0] if has_vfirst else []) + [k_k, k_a, r_k, gn_g, gn_b]
    for a in vecs:
        in_specs.append(_const_spec((1, a.size)))
        args.append(vec(a))
    mats = [w2p, a2p] + ([v2p] if has_vfirst else [])
    for m in mats:
        in_specs.append(_const_spec(m.shape))
        args.append(m)
    out_specs = [row(d)]
    out_shape = [jax.ShapeDtypeStruct((n, d), BF16)]
    if not has_vfirst:
        out_specs.append(row(d))
        out_shape.append(jax.ShapeDtypeStruct((n, d), F32))
    outs = pl.pallas_call(
        functools.partial(_rwkv_body, has_vfirst=has_vfirst, n_sub=tc // SCAN_CHUNK),
        grid=(batch, nt),
        in_specs=in_specs,
        out_specs=out_specs,
        out_shape=out_shape,
        scratch_shapes=[pltpu.VMEM((V7X_SUBLANES, d3), F32), pltpu.VMEM((V7X_SUBLANES, lw), F32),
                        pltpu.VMEM((d // TILE, TILE, TILE), F32)],
        compiler_params=_params("parallel", "arbitrary"),
        name="rwkv7_first" if not has_vfirst else "rwkv7_rest",
    )(*args)
    if has_vfirst:
        return outs[0], vfirst
    return outs[0], outs[1]


def _merge_body(x_ref, ga_ref, gb_ref, gu_ref, vln_ref, yb_ref, ws_ref, bias_ref, wa_ref, wb_ref, wo_ref,
                o_ref, ya_ref):
    tm, d = x_ref.shape
    for c in range(tm // SGU_CHUNK):
        rows = slice(c * SGU_CHUNK, (c + 1) * SGU_CHUNK)
        for g in range(d // SGU_GROUP_DIM):
            cols = slice(g * SGU_GROUP_DIM, (g + 1) * SGU_GROUP_DIM)
            s = _dot(ws_ref[g], vln_ref[rows, cols]) + bias_ref[:, cols]
            ya_ref[rows, cols] = (gu_ref[rows, cols].astype(F32) * s).astype(BF16)
    pa = _dot(ya_ref[...], wa_ref[...])
    pb = _dot(yb_ref[...], wb_ref[...])
    merged = jax.nn.sigmoid(ga_ref[...]) * pa + jax.nn.sigmoid(gb_ref[...]) * pb
    o_ref[...] = x_ref[...] + _dot(merged.astype(BF16), wo_ref[...])


def _merge(x, ga, gb, gu, vln, yb, ws, bias, wa, wb, wo, *, tm=512):
    n, d = x.shape
    row = pl.BlockSpec((tm, d), lambda i: (i, 0))
    return pl.pallas_call(
        _merge_body,
        grid=(n // tm,),
        in_specs=[row] * 6 + [_const_spec(ws.shape), _const_spec(bias.shape), _const_spec(wa.shape),
                              _const_spec(wb.shape), _const_spec(wo.shape)],
        out_specs=row,
        out_shape=jax.ShapeDtypeStruct((n, d), F32),
        scratch_shapes=[pltpu.VMEM((tm, d), BF16)],
        compiler_params=_params("parallel"),
        name="merge",
    )(x, ga, gb, gu, vln, yb, ws, bias, wa, wb, wo)


def _pad_rows(w, offset, rows):
    out = jnp.zeros((rows, w.shape[1]), BF16)
    return lax.dynamic_update_slice(out, w.astype(BF16), (offset, 0))


def kernel(x, w_in_first, mu_first, w_in_rest, mu_rest, rwkv_v0, rwkv_v2, ffn1_norm, ffn1_w_gu, ffn1_w_down, mix_norm, sgu_ln_g, sgu_ln_b, sgu_w_s, sgu_b_s, rwkv_w0, rwkv_w2, rwkv_a0, rwkv_a2, rwkv_k_k, rwkv_k_a, rwkv_r_k, rwkv_gn_g, rwkv_gn_b, w_proj_a, w_proj_b, w_out, ffn2_norm, ffn2_w_gu, ffn2_w_down, final_norm):
    batch, seq, d = x.shape
    depth = ffn1_norm.shape[0]
    n = batch * seq
    assert d % TILE == 0 and seq % 512 == 0 and n % 512 == 0
    assert sgu_w_s.shape[1:] == (d // SGU_GROUP_DIM, SGU_CHUNK, SGU_CHUNK)
    d_lora = rwkv_w2.shape[1]
    a_lora = rwkv_a2.shape[1]
    lora_w = TILE
    causal = jnp.tril(jnp.ones((SGU_CHUNK, SGU_CHUNK), dtype=bool))

    xs = x.reshape(n, d)
    v_first = None
    for layer in range(depth):
        xs = _ffn(xs, ffn1_norm[layer], ffn1_w_gu[layer].astype(BF16), ffn1_w_down[layer].astype(BF16))

        if layer == 0:
            w_in, mu = w_in_first, mu_first
        else:
            w_in, mu = w_in_rest[layer - 1], mu_rest[layer - 1]
        n_low = w_in.shape[1] - 7 * d
        assert n_low <= lora_w
        w_main = w_in[:, :7 * d].astype(BF16)
        w_lora = jnp.pad(w_in[:, 7 * d:], ((0, 0), (0, lora_w - n_low))).astype(BF16)
        mu_rkv = mu[:3 * d]
        mu_lora = jnp.pad(mu[3 * d:], (0, lora_w - n_low))
        ga, gb, gu, vln, prkv, plora = _inproj(xs, mix_norm[layer], w_main, w_lora,
                                               sgu_ln_g[layer], sgu_ln_b[layer])

        w2p = _pad_rows(rwkv_w2[layer], 0, lora_w)
        a2p = _pad_rows(rwkv_a2[layer], d_lora, lora_w)
        if layer == 0:
            v0, v2p = None, None
        else:
            v0 = rwkv_v0[layer - 1]
            v2p = _pad_rows(rwkv_v2[layer - 1], d_lora + a_lora, lora_w)
        yb, v_first = _rwkv(prkv, plora, v_first, mu_rkv, mu_lora, rwkv_w0[layer], rwkv_a0[layer], v0,
                            rwkv_k_k[layer], rwkv_k_a[layer], rwkv_r_k[layer].reshape(-1),
                            rwkv_gn_g[layer], rwkv_gn_b[layer], w2p, a2p, v2p, batch=batch, seq=seq)

        ws = jnp.where(causal, sgu_w_s[layer], 0).astype(BF16)
        bias = jnp.repeat(sgu_b_s[layer].T, SGU_GROUP_DIM, axis=1)
        xs = _merge(xs, ga, gb, gu, vln, yb, ws, bias, w_proj_a[layer].astype(BF16),
                    w_proj_b[layer].astype(BF16), w_out[layer].astype(BF16))

        final_g = final_norm if layer == depth - 1 else None
        xs = _ffn(xs, ffn2_norm[layer], ffn2_w_gu[layer].astype(BF16), ffn2_w_down[layer].astype(BF16), final_g)
    return xs.reshape(batch, seq, d)
```

```python
import functools

import jax
import jax.numpy as jnp
from jax import lax
from jax.experimental import pallas as pl
from jax.experimental.pallas import tpu as pltpu

F32 = jnp.float32
BF16 = jnp.bfloat16

NORM_EPS = 1e-6
LN_EPS = 1e-5
GN_EPS = 64e-5

SGU_CHUNK = 128
SGU_GROUP_DIM = 128
HEAD_DIM = 64

V7X_MXU_DIM = 256
V7X_LANES = 128
V7X_SUBLANES = 8
V7X_VMEM_LIMIT_BYTES = 60000 * 1024

SCAN_CHUNK = 64
TILE = V7X_MXU_DIM
HEADS_PER_TILE = TILE // HEAD_DIM
HEADS_PER_VREG = V7X_LANES // HEAD_DIM
INV_LEVELS = tuple(2 ** i for i in range(1, SCAN_CHUNK.bit_length() - 1))


def _params(*semantics):
    return pltpu.CompilerParams(dimension_semantics=semantics, vmem_limit_bytes=V7X_VMEM_LIMIT_BYTES)


def _const_spec(shape):
    return pl.BlockSpec(shape, lambda *_: (0,) * len(shape), pipeline_mode=pl.Buffered(1))


def _rms(x, g):
    return x * lax.rsqrt(jnp.mean(x * x, axis=-1, keepdims=True) + NORM_EPS) * g


def _dot(a, b):
    return jnp.dot(a, b, preferred_element_type=F32)


def _dot_nt(a, b):
    return lax.dot_general(a, b, (((1,), (1,)), ((), ())), preferred_element_type=F32)


def _dot_tn(a, b):
    return lax.dot_general(a, b, (((0,), (0,)), ((), ())), preferred_element_type=F32)


def _ffn_body(*refs, d_ff, f_chunk, final):
    if final:
        x_ref, g_ref, wgu_ref, wd_ref, fg_ref, o_ref = refs
    else:
        x_ref, g_ref, wgu_ref, wd_ref, o_ref = refs
    x = x_ref[...]
    h = _rms(x, g_ref[...]).astype(BF16)
    acc = jnp.zeros_like(x)
    for c in range(d_ff // f_chunk):
        lo = c * f_chunk
        gate = _dot(h, wgu_ref[:, lo:lo + f_chunk])
        up = _dot(h, wgu_ref[:, d_ff + lo:d_ff + lo + f_chunk])
        act = (gate * jax.nn.sigmoid(gate) * up).astype(BF16)
        acc = acc + _dot(act, wd_ref[lo:lo + f_chunk, :])
    y = x + 0.5 * acc
    if final:
        y = _rms(y, fg_ref[...])
    o_ref[...] = y


def _ffn(x, g, wgu, wd, final_g=None, *, tm=512):
    n, d = x.shape
    d_ff = wd.shape[0]
    final = final_g is not None
    row = pl.BlockSpec((tm, d), lambda i: (i, 0))
    in_specs = [row, _const_spec((1, d)), _const_spec(wgu.shape), _const_spec(wd.shape)]
    args = [x, g.reshape(1, d), wgu, wd]
    if final:
        in_specs.append(_const_spec((1, d)))
        args.append(final_g.reshape(1, d))
    return pl.pallas_call(
        functools.partial(_ffn_body, d_ff=d_ff, f_chunk=V7X_MXU_DIM, final=final),
        grid=(n // tm,),
        in_specs=in_specs,
        out_specs=row,
        out_shape=jax.ShapeDtypeStruct((n, d), F32),
        compiler_params=_params("parallel"),
        name="ffn_final" if final else "ffn",
    )(*args)


def _inproj_body(x_ref, g_ref, w_ref, wl_ref, lng_ref, lnb_ref,
                 ga_ref, gb_ref, gu_ref, vln_ref, prkv_ref, plora_ref):
    d = x_ref.shape[1]
    h = _rms(x_ref[...], g_ref[...]).astype(BF16)
    ga_ref[...] = _dot(h, w_ref[:, 0:d])
    gb_ref[...] = _dot(h, w_ref[:, d:2 * d])
    gu_ref[...] = jax.nn.gelu(_dot(h, w_ref[:, 2 * d:3 * d])).astype(BF16)
    va = jax.nn.gelu(_dot(h, w_ref[:, 3 * d:4 * d]))
    mu = jnp.mean(va, axis=-1, keepdims=True)
    vc = va - mu
    var = jnp.mean(vc * vc, axis=-1, keepdims=True)
    vln_ref[...] = (vc * lax.rsqrt(var + LN_EPS) * lng_ref[...] + lnb_ref[...]).astype(BF16)
    for c in range(3):
        prkv_ref[:, c * d:(c + 1) * d] = _dot(h, w_ref[:, (4 + c) * d:(5 + c) * d])
    plora_ref[...] = _dot(h, wl_ref[...])


def _inproj(x, g, w_main, w_lora, ln_g, ln_b, *, tm=512):
    n, d = x.shape
    lw = w_lora.shape[1]
    row = lambda w: pl.BlockSpec((tm, w), lambda i: (i, 0))
    return pl.pallas_call(
        _inproj_body,
        grid=(n // tm,),
        in_specs=[row(d), _const_spec((1, d)), _const_spec(w_main.shape), _const_spec(w_lora.shape),
                  _const_spec((1, d)), _const_spec((1, d))],
        out_specs=[row(d), row(d), row(d), row(d), row(3 * d), row(lw)],
        out_shape=[jax.ShapeDtypeStruct((n, d), F32), jax.ShapeDtypeStruct((n, d), F32),
                   jax.ShapeDtypeStruct((n, d), BF16), jax.ShapeDtypeStruct((n, d), BF16),
                   jax.ShapeDtypeStruct((n, 3 * d), F32), jax.ShapeDtypeStruct((n, lw), F32)],
        compiler_params=_params("parallel"),
        name="inproj",
    )(x, g.reshape(1, d), w_main, w_lora, ln_g.reshape(1, d), ln_b.reshape(1, d))


def _split_bf16(x, terms):
    parts = []
    for i in range(terms):
        p = x.astype(BF16)
        parts.append(p)
        if i + 1 < terms:
            x = x - p.astype(F32)
    return parts


def _interleave(main, *others, period=3):
    for i, _ in enumerate(main):
        if i % period == 0:
            for o in others:
                next(o, None)
    for o in others:
        for _ in o:
            pass


def _rwkv_body(*refs, has_vfirst, n_sub):
    if has_vfirst:
        (prkv_ref, plora_ref, vf_ref, mu_ref, mul_ref, w0_ref, a0_ref, v0_ref, kk_ref, ka_ref, rk_ref,
         gng_ref, gnb_ref, w2_ref, a2_ref, v2_ref, yb_ref, *scratch) = refs
    else:
        (prkv_ref, plora_ref, mu_ref, mul_ref, w0_ref, a0_ref, kk_ref, ka_ref, rk_ref,
         gng_ref, gnb_ref, w2_ref, a2_ref, yb_ref, vfo_ref, *scratch) = refs
    (carry_ref, carryl_ref, s_ref, mask_ref, ones_ref, tril_ref,
     x_scr, z_scr, bk_scr, v_scr, coef_scr, wall_scr) = scratch
    C = SCAN_CHUNK
    d = yb_ref.shape[1]
    tiles = range(d // TILE)
    sls = [slice(q * TILE, (q + 1) * TILE) for q in tiles]

    @pl.when(pl.program_id(1) == 0)
    def _():
        carry_ref[...] = jnp.zeros_like(carry_ref)
        carryl_ref[...] = jnp.zeros_like(carryl_ref)
        s_ref[...] = jnp.zeros_like(s_ref)

    def iota(shape, dim):
        return lax.broadcasted_iota(jnp.int32, shape, dim)

    def level_mask(t, c, s):
        return (t // (2 * s) == c // (2 * s)) & (t % (2 * s) >= s) & (c % (2 * s) < s)

    def as_bf16(mask):
        return jnp.where(mask, 1.0, 0.0).astype(BF16)

    row_v = iota((C, V7X_LANES), 0)
    lane_v = iota((C, V7X_LANES), 1)
    for hf in range(HEADS_PER_VREG):
        mask_ref[hf] = as_bf16(lane_v // HEAD_DIM == hf)
        for li, s in enumerate(INV_LEVELS):
            mask_ref[2 + 2 * li + hf] = as_bf16(
                (lane_v // HEAD_DIM == hf) & level_mask(row_v, lane_v % HEAD_DIM, s))
    ones_bd = iota((TILE, TILE), 0) // HEAD_DIM == iota((TILE, TILE), 1) // HEAD_DIM
    ones_ref[...] = as_bf16(ones_bd)
    tril_ref[...] = as_bf16(iota((C, C), 0) >= iota((C, C), 1))

    tt = iota((C, TILE), 0)
    cc = iota((C, TILE), 1) % HEAD_DIM
    strict = cc < tt
    incl = cc <= tt
    incl2 = jnp.concatenate([incl, incl], axis=1)
    eye = jnp.where(cc == tt, 1.0, 0.0)
    first_row = iota((V7X_SUBLANES, 1), 0) == 0
    zero_v = jnp.zeros((C, V7X_LANES), BF16)

    def bstack(mb, mi):
        blocks = []
        for h in range(HEADS_PER_TILE):
            t, hf = divmod(h, HEADS_PER_VREG)
            piece = mb[:, t * V7X_LANES:(t + 1) * V7X_LANES] * mask_ref[mi + hf]
            blocks.append(jnp.concatenate(
                [piece if i == t else zero_v for i in range(TILE // V7X_LANES)], axis=1))
        return jnp.concatenate(blocks, axis=0)

    def head_sum(x):
        parts = [p for sl in sls for p in _split_bf16(x[:, sl], 2)]
        s = _dot(jnp.concatenate(parts, axis=0), ones_ref[...])
        return jnp.concatenate(
            [s[2 * q * C:(2 * q + 1) * C] + s[(2 * q + 1) * C:(2 * q + 2) * C] for q in tiles], axis=1)

    def shift_mix(ref, cref, mu, rows, cols):
        p = ref[rows, cols]
        rolled = pltpu.roll(p, 1, 0)
        first = jnp.where(first_row, cref[V7X_SUBLANES - 1:V7X_SUBLANES, cols], rolled[0:V7X_SUBLANES])
        prev = jnp.concatenate([first, rolled[V7X_SUBLANES:]], axis=0)
        cref[:, cols] = p[C - V7X_SUBLANES:C]
        return p + mu[:, cols] * (prev - p)

    def chunk_rows(j):
        return pl.ds(pl.multiple_of(j * C, C), C)

    def prep(j, out):
        rows = chunk_rows(j)
        plm = shift_mix(plora_ref, carryl_ref, mul_ref, rows, slice(None))
        pl_wa = plm[:, 0:V7X_LANES]
        lw = _dot(jnp.tanh(pl_wa).astype(BF16), w2_ref[...])
        la = _dot(pl_wa.astype(BF16), a2_ref[...])
        if has_vfirst:
            lv = _dot(plm[:, V7X_LANES:2 * V7X_LANES].astype(BF16), v2_ref[...])
        yield
        z = -(w0_ref[...] + lw)
        softplus = jnp.maximum(z, 0.0) + jnp.log(1.0 + jnp.exp(-jnp.abs(z)))
        ld = -jnp.exp(-softplus - 0.5)
        a = jax.nn.sigmoid(a0_ref[...] + la)
        r = shift_mix(prkv_ref, carry_ref, mu_ref, rows, slice(0, d))
        k = shift_mix(prkv_ref, carry_ref, mu_ref, rows, slice(d, 2 * d))
        v = shift_mix(prkv_ref, carry_ref, mu_ref, rows, slice(2 * d, 3 * d))
        if has_vfirst:
            v = v + (vf_ref[rows, :] - v) * jax.nn.sigmoid(v0_ref[...] + lv)
        kk = k * kk_ref[...]
        n2 = head_sum(kk * kk)
        cl = sum(_dot(tril_ref[...], p) for p in _split_bf16(ld, 3))
        yield
        kk = kk * lax.rsqrt(jnp.maximum(n2, 1e-24))
        k = k * (1.0 + (a - 1.0) * ka_ref[...])
        b = kk * a
        out["coef"] = head_sum(r * k * rk_ref[...])
        yield
        cl_last = cl[C - 1:C, :]
        w_inv = jnp.exp(-cl)
        w_rem = jnp.exp(cl_last - cl)
        out["x"] = jnp.concatenate([kk * jnp.exp(cl - ld), r * jnp.exp(cl)], axis=0).astype(BF16)
        out["z"] = jnp.concatenate([b * w_inv, k * w_inv], axis=0).astype(BF16)
        out["bk"] = jnp.concatenate([b * w_rem, k * w_rem], axis=0).astype(BF16)
        out["v"] = v
        out["wall"] = jnp.exp(cl_last)
        yield

    def store_prepared(slot, p):
        x_scr[slot] = p["x"]
        z_scr[slot] = p["z"]
        bk_scr[slot] = p["bk"]
        v_scr[slot] = p["v"]
        coef_scr[slot] = p["coef"]
        wall_scr[slot] = jnp.broadcast_to(p["wall"], wall_scr.shape[1:])

    def load_prepared(slot):
        return {"x": x_scr[slot], "z": z_scr[slot], "bk": bk_scr[slot], "v": v_scr[slot],
                "coef": coef_scr[slot], "wall": wall_scr[slot, 0:1, :]}

    def phase1(ps):
        jobs = [(p, sl) for p in ps for sl in sls]
        n = range(len(jobs))
        x = [p["x"][:, sl] for p, sl in jobs]
        zq = [jnp.concatenate([bstack(p["z"][0:C, sl], 0), bstack(p["z"][C:2 * C, sl], 0)], axis=0)
              for p, sl in jobs]
        am = [_dot(x[i], zq[i].T) for i in n]
        yield
        nq = [jnp.where(strict, am[i][0:C, 0:TILE], 0.0) for i in n]
        nqb = [m.astype(BF16) for m in nq]
        vb = [p["v"][:, sl].astype(BF16) for p, sl in jobs]
        vstack = [bstack(vb[i], 0) for i in n]
        akv = [_dot(jnp.where(strict, am[i][0:C, TILE:2 * TILE], 0.0).astype(BF16), vstack[i]) for i in n]
        yield
        inv = [eye - jnp.where(level_mask(tt, cc, 1), nq[i], 0.0) for i in n]
        for li in range(len(INV_LEVELS)):
            invb = [m.astype(BF16) for m in inv]
            pe = [_dot(invb[i], bstack(nqb[i], 2 + 2 * li)) for i in n]
            yield
            inv = [inv[i] - _dot(pe[i].astype(BF16), bstack(invb[i], 0)) for i in n]
            yield
        invb = [m.astype(BF16) for m in inv]
        pk = [_dot(invb[i], bstack(x[i][0:C], 0)).astype(BF16) for i in n]
        u0 = [-_dot(invb[i], bstack(akv[i].astype(BF16), 0)) for i in n]
        yield
        abot = [jnp.where(incl2, am[i][C:2 * C, :], 0.0).astype(BF16) for i in n]
        rq = [(x[i][C:2 * C].astype(F32) - _dot(abot[i][:, 0:TILE], bstack(pk[i], 0))).astype(BF16) for i in n]
        y0 = [_dot(abot[i], jnp.concatenate([bstack(u0[i].astype(BF16), 0), vstack[i]], axis=0)) for i in n]
        for i, (p, sl) in enumerate(jobs):
            for name, val in (("pk", pk[i]), ("u0", u0[i]), ("rq", rq[i]), ("y0", y0[i]), ("vb", vb[i])):
                p.setdefault(name, []).append(val)
        yield

    def phase2(p):
        s0 = [s_ref[q] for q in tiles]
        s0b = [m.astype(BF16) for m in s0]
        gs = [_dot_nt(jnp.concatenate([p["pk"][q], p["rq"][q]], axis=0), s0b[q]) for q in tiles]
        ub = [(p["u0"][q] - gs[q][0:C]).astype(BF16) for q in tiles]
        p["y"] = jnp.concatenate([p["y0"][q] + gs[q][C:2 * C] for q in tiles], axis=1)
        yield
        for q in tiles:
            uv = jnp.concatenate([ub[q], p["vb"][q]], axis=0)
            upd = _dot_tn(uv, p["bk"][:, sls[q]])
            s_ref[q] = s0[q] * p["wall"][:, sls[q]] + jnp.where(ones_bd, upd, 0.0)
        yield

    def post(j, p):
        rows = chunk_rows(j)
        y = p["y"]
        if not has_vfirst:
            vfo_ref[rows, :] = p["v"]
        mean = head_sum(y) * (1.0 / HEAD_DIM)
        yield
        yc = y - mean
        var = head_sum(yc * yc) * (1.0 / HEAD_DIM)
        yield
        yn = yc * lax.rsqrt(var + GN_EPS) * gng_ref[...] + gnb_ref[...]
        yb_ref[rows, :] = (yn + p["coef"] * p["v"]).astype(BF16)
        yield

    def pair(i, carry):
        pa, pb = load_prepared(0), load_prepared(1)
        nxt = [{}, {}]

        def chain():
            yield from phase1([pa, pb])
            yield from phase2(pa)
            post_a = post(2 * i, pa)
            for _ in phase2(pb):
                next(post_a, None)
                yield
            yield from post_a
            yield from post(2 * i + 1, pb)

        def prepare_next():
            for k in range(2):
                yield from prep(jnp.minimum(2 * i + 2 + k, n_sub - 1), nxt[k])

        _interleave(chain(), prepare_next(), period=2)
        for k in range(2):
            store_prepared(k, nxt[k])
        return carry

    for k in range(2):
        first = {}
        _interleave(prep(k, first))
        store_prepared(k, first)
    lax.fori_loop(0, n_sub // 2, pair, 0)


def _rwkv(prkv, plora, vfirst, mu_rkv, mu_lora, w0, a0, v0, k_k, k_a, r_k, gn_g, gn_b, w2p, a2p, v2p,
          *, batch, seq, tc=1024):
    n, d3 = prkv.shape
    d = d3 // 3
    lw = plora.shape[1]
    nt = seq // tc
    n_sub = tc // SCAN_CHUNK
    assert n_sub % 2 == 0 and HEADS_PER_VREG == 2
    has_vfirst = vfirst is not None
    row = lambda w: pl.BlockSpec((tc, w), lambda b, t: (b * nt + t, 0))
    vec = lambda a: a.reshape(1, -1)
    in_specs = [row(d3), row(lw)]
    args = [prkv, plora]
    if has_vfirst:
        in_specs.append(row(d))
        args.append(vfirst)
    vecs = [mu_rkv, mu_lora, w0, a0] + ([v0] if has_vfirst else []) + [k_k, k_a, r_k, gn_g, gn_b]
    for a in vecs:
        in_specs.append(_const_spec((1, a.size)))
        args.append(vec(a))
    mats = [w2p, a2p] + ([v2p] if has_vfirst else [])
    for m in mats:
        in_specs.append(_const_spec(m.shape))
        args.append(m)
    out_specs = [row(d)]
    out_shape = [jax.ShapeDtypeStruct((n, d), BF16)]
    if not has_vfirst:
        out_specs.append(row(d))
        out_shape.append(jax.ShapeDtypeStruct((n, d), F32))
    C = SCAN_CHUNK
    scratch = [
        pltpu.VMEM((V7X_SUBLANES, d3), F32),
        pltpu.VMEM((V7X_SUBLANES, lw), F32),
        pltpu.VMEM((d // TILE, TILE, TILE), F32),
        pltpu.VMEM((2 + 2 * len(INV_LEVELS), C, V7X_LANES), BF16),
        pltpu.VMEM((TILE, TILE), BF16),
        pltpu.VMEM((C, C), BF16),
        pltpu.VMEM((2, 2 * C, d), BF16),
        pltpu.VMEM((2, 2 * C, d), BF16),
        pltpu.VMEM((2, 2 * C, d), BF16),
        pltpu.VMEM((2, C, d), F32),
        pltpu.VMEM((2, C, d), F32),
        pltpu.VMEM((2, V7X_SUBLANES, d), F32),
    ]
    outs = pl.pallas_call(
        functools.partial(_rwkv_body, has_vfirst=has_vfirst, n_sub=n_sub),
        grid=(batch, nt),
        in_specs=in_specs,
        out_specs=out_specs,
        out_shape=out_shape,
        scratch_shapes=scratch,
        compiler_params=_params("parallel", "arbitrary"),
        name="rwkv7_first" if not has_vfirst else "rwkv7_rest",
    )(*args)
    if has_vfirst:
        return outs[0], vfirst
    return outs[0], outs[1]


def _merge_body(x_ref, ga_ref, gb_ref, gu_ref, vln_ref, yb_ref, ws_ref, bias_ref, wa_ref, wb_ref, wo_ref,
                o_ref, ya_ref):
    tm, d = x_ref.shape
    for c in range(tm // SGU_CHUNK):
        rows = slice(c * SGU_CHUNK, (c + 1) * SGU_CHUNK)
        for g in range(d // SGU_GROUP_DIM):
            cols = slice(g * SGU_GROUP_DIM, (g + 1) * SGU_GROUP_DIM)
            s = _dot(ws_ref[g], vln_ref[rows, cols]) + bias_ref[:, cols]
            ya_ref[rows, cols] = (gu_ref[rows, cols].astype(F32) * s).astype(BF16)
    pa = _dot(ya_ref[...], wa_ref[...])
    pb = _dot(yb_ref[...], wb_ref[...])
    merged = jax.nn.sigmoid(ga_ref[...]) * pa + jax.nn.sigmoid(gb_ref[...]) * pb
    o_ref[...] = x_ref[...] + _dot(merged.astype(BF16), wo_ref[...])


def _merge(x, ga, gb, gu, vln, yb, ws, bias, wa, wb, wo, *, tm=512):
    n, d = x.shape
    row = pl.BlockSpec((tm, d), lambda i: (i, 0))
    return pl.pallas_call(
        _merge_body,
        grid=(n // tm,),
        in_specs=[row] * 6 + [_const_spec(ws.shape), _const_spec(bias.shape), _const_spec(wa.shape),
                              _const_spec(wb.shape), _const_spec(wo.shape)],
        out_specs=row,
        out_shape=jax.ShapeDtypeStruct((n, d), F32),
        scratch_shapes=[pltpu.VMEM((tm, d), BF16)],
        compiler_params=_params("parallel"),
        name="merge",
    )(x, ga, gb, gu, vln, yb, ws, bias, wa, wb, wo)


def _pad_rows(w, offset, rows):
    out = jnp.zeros((rows, w.shape[1]), BF16)
    return lax.dynamic_update_slice(out, w.astype(BF16), (offset, 0))


def kernel(x, w_in_first, mu_first, w_in_rest, mu_rest, rwkv_v0, rwkv_v2, ffn1_norm, ffn1_w_gu, ffn1_w_down, mix_norm, sgu_ln_g, sgu_ln_b, sgu_w_s, sgu_b_s, rwkv_w0, rwkv_w2, rwkv_a0, rwkv_a2, rwkv_k_k, rwkv_k_a, rwkv_r_k, rwkv_gn_g, rwkv_gn_b, w_proj_a, w_proj_b, w_out, ffn2_norm, ffn2_w_gu, ffn2_w_down, final_norm):
    batch, seq, d = x.shape
    depth = ffn1_norm.shape[0]
    n = batch * seq
    assert d % TILE == 0 and seq % 1024 == 0
    assert sgu_w_s.shape[1:] == (d // SGU_GROUP_DIM, SGU_CHUNK, SGU_CHUNK)
    d_lora = rwkv_w2.shape[1]
    a_lora = rwkv_a2.shape[1]
    lora_w = TILE
    assert d_lora + a_lora == V7X_LANES and rwkv_v2.shape[1] <= V7X_LANES
    causal = jnp.tril(jnp.ones((SGU_CHUNK, SGU_CHUNK), dtype=bool))

    xs = x.reshape(n, d)
    v_first = None
    for layer in range(depth):
        xs = _ffn(xs, ffn1_norm[layer], ffn1_w_gu[layer].astype(BF16), ffn1_w_down[layer].astype(BF16))

        if layer == 0:
            w_in, mu = w_in_first, mu_first
        else:
            w_in, mu = w_in_rest[layer - 1], mu_rest[layer - 1]
        n_low = w_in.shape[1] - 7 * d
        assert n_low <= lora_w
        w_main = w_in[:, :7 * d].astype(BF16)
        w_lora = jnp.pad(w_in[:, 7 * d:], ((0, 0), (0, lora_w - n_low))).astype(BF16)
        mu_rkv = mu[:3 * d]
        mu_lora = jnp.pad(mu[3 * d:], (0, lora_w - n_low))
        ga, gb, gu, vln, prkv, plora = _inproj(xs, mix_norm[layer], w_main, w_lora,
                                               sgu_ln_g[layer], sgu_ln_b[layer])

        w2p = _pad_rows(rwkv_w2[layer], 0, V7X_LANES)
        a2p = _pad_rows(rwkv_a2[layer], d_lora, V7X_LANES)
        if layer == 0:
            v0, v2p = None, None
        else:
            v0 = rwkv_v0[layer - 1]
            v2p = _pad_rows(rwkv_v2[layer - 1], 0, V7X_LANES)
        yb, v_first = _rwkv(prkv, plora, v_first, mu_rkv, mu_lora, rwkv_w0[layer], rwkv_a0[layer], v0,
                            rwkv_k_k[layer], rwkv_k_a[layer], rwkv_r_k[layer].reshape(-1),
                            rwkv_gn_g[layer], rwkv_gn_b[layer], w2p, a2p, v2p, batch=batch, seq=seq)

        ws = jnp.where(causal, sgu_w_s[layer], 0).astype(BF16)
        bias = jnp.repeat(sgu_b_s[layer].T, SGU_GROUP_DIM, axis=1)
        xs = _merge(xs, ga, gb, gu, vln, yb, ws, bias, w_proj_a[layer].astype(BF16),
                    w_proj_b[layer].astype(BF16), w_out[layer].astype(BF16))

        final_g = final_norm if layer == depth - 1 else None
        xs = _ffn(xs, ffn2_norm[layer], ffn2_w_gu[layer].astype(BF16), ffn2_w_down[layer].astype(BF16), final_g)
    return xs.reshape(batch, seq, d)
```

```python
import functools
import math

import jax
import jax.numpy as jnp
from jax import lax
from jax.experimental import pallas as pl
from jax.experimental.pallas import tpu as pltpu

F32 = jnp.float32
BF16 = jnp.bfloat16

NORM_EPS = 1e-6
LN_EPS = 1e-5
GN_EPS = 64e-5

SGU_CHUNK = 128
SGU_GROUP_DIM = 128
HEAD_DIM = 64
DECAY_SCALE = math.exp(-0.5)

V7X_MXU_DIM = 256
V7X_LANES = 128
V7X_SUBLANES = 8
V7X_VMEM_LIMIT_BYTES = 60000 * 1024

SCAN_CHUNK = 64
TILE = V7X_MXU_DIM
HEADS_PER_TILE = TILE // HEAD_DIM
HEADS_PER_VREG = V7X_LANES // HEAD_DIM
INV_LEVELS = tuple(2 ** i for i in range(1, SCAN_CHUNK.bit_length() - 1))


def _params(*semantics):
    return pltpu.CompilerParams(dimension_semantics=semantics, vmem_limit_bytes=V7X_VMEM_LIMIT_BYTES)


def _const_spec(shape):
    return pl.BlockSpec(shape, lambda *_: (0,) * len(shape), pipeline_mode=pl.Buffered(1))


def _rms(x, g):
    return x * lax.rsqrt(jnp.mean(x * x, axis=-1, keepdims=True) + NORM_EPS) * g


def _dot(a, b):
    return jnp.dot(a, b, preferred_element_type=F32)


def _dot_nt(a, b):
    return lax.dot_general(a, b, (((1,), (1,)), ((), ())), preferred_element_type=F32)


def _dot_tn(a, b):
    return lax.dot_general(a, b, (((0,), (0,)), ((), ())), preferred_element_type=F32)


def _ffn_body(*refs, d_ff, f_chunk, final):
    if final:
        x_ref, g_ref, wgu_ref, wd_ref, fg_ref, o_ref = refs
    else:
        x_ref, g_ref, wgu_ref, wd_ref, o_ref = refs
    x = x_ref[...]
    h = _rms(x, g_ref[...]).astype(BF16)
    acc = jnp.zeros_like(x)
    for c in range(d_ff // f_chunk):
        lo = c * f_chunk
        gate = _dot(h, wgu_ref[:, lo:lo + f_chunk])
        up = _dot(h, wgu_ref[:, d_ff + lo:d_ff + lo + f_chunk])
        act = (gate * jax.nn.sigmoid(gate) * up).astype(BF16)
        acc = acc + _dot(act, wd_ref[lo:lo + f_chunk, :])
    y = x + 0.5 * acc
    if final:
        y = _rms(y, fg_ref[...])
    o_ref[...] = y


def _ffn(x, g, wgu, wd, final_g=None, *, tm=512):
    n, d = x.shape
    d_ff = wd.shape[0]
    final = final_g is not None
    row = pl.BlockSpec((tm, d), lambda i: (i, 0))
    in_specs = [row, _const_spec((1, d)), _const_spec(wgu.shape), _const_spec(wd.shape)]
    args = [x, g.reshape(1, d), wgu, wd]
    if final:
        in_specs.append(_const_spec((1, d)))
        args.append(final_g.reshape(1, d))
    return pl.pallas_call(
        functools.partial(_ffn_body, d_ff=d_ff, f_chunk=V7X_MXU_DIM, final=final),
        grid=(n // tm,),
        in_specs=in_specs,
        out_specs=row,
        out_shape=jax.ShapeDtypeStruct((n, d), F32),
        compiler_params=_params("parallel"),
        name="ffn_final" if final else "ffn",
    )(*args)


def _inproj_body(x_ref, g_ref, w_ref, wl_ref, lng_ref, lnb_ref,
                 ga_ref, gb_ref, gu_ref, vln_ref, prkv_ref, plora_ref):
    d = x_ref.shape[1]
    h = _rms(x_ref[...], g_ref[...]).astype(BF16)
    ga_ref[...] = _dot(h, w_ref[:, 0:d])
    gb_ref[...] = _dot(h, w_ref[:, d:2 * d])
    gu_ref[...] = jax.nn.gelu(_dot(h, w_ref[:, 2 * d:3 * d])).astype(BF16)
    va = jax.nn.gelu(_dot(h, w_ref[:, 3 * d:4 * d]))
    mu = jnp.mean(va, axis=-1, keepdims=True)
    vc = va - mu
    var = jnp.mean(vc * vc, axis=-1, keepdims=True)
    vln_ref[...] = (vc * lax.rsqrt(var + LN_EPS) * lng_ref[...] + lnb_ref[...]).astype(BF16)
    for c in range(3):
        prkv_ref[:, c * d:(c + 1) * d] = _dot(h, w_ref[:, (4 + c) * d:(5 + c) * d])
    plora_ref[...] = _dot(h, wl_ref[...])


def _inproj(x, g, w_main, w_lora, ln_g, ln_b, *, tm=512):
    n, d = x.shape
    lw = w_lora.shape[1]
    row = lambda w: pl.BlockSpec((tm, w), lambda i: (i, 0))
    return pl.pallas_call(
        _inproj_body,
        grid=(n // tm,),
        in_specs=[row(d), _const_spec((1, d)), _const_spec(w_main.shape), _const_spec(w_lora.shape),
                  _const_spec((1, d)), _const_spec((1, d))],
        out_specs=[row(d), row(d), row(d), row(d), row(3 * d), row(lw)],
        out_shape=[jax.ShapeDtypeStruct((n, d), F32), jax.ShapeDtypeStruct((n, d), F32),
                   jax.ShapeDtypeStruct((n, d), BF16), jax.ShapeDtypeStruct((n, d), BF16),
                   jax.ShapeDtypeStruct((n, 3 * d), F32), jax.ShapeDtypeStruct((n, lw), F32)],
        compiler_params=_params("parallel"),
        name="inproj",
    )(x, g.reshape(1, d), w_main, w_lora, ln_g.reshape(1, d), ln_b.reshape(1, d))


def _split_bf16(x, terms):
    parts = []
    for i in range(terms):
        p = x.astype(BF16)
        parts.append(p)
        if i + 1 < terms:
            x = x - p.astype(F32)
    return parts


def _interleave(main, *others, period=3):
    for i, _ in enumerate(main):
        if i % period == 0:
            for o in others:
                next(o, None)
    for o in others:
        for _ in o:
            pass


def _rwkv_body(*refs, has_vfirst, n_sub):
    if has_vfirst:
        (prkv_ref, plora_ref, vf_ref, mu_ref, mul_ref, w0_ref, a0_ref, v0_ref, kk_ref, ka_ref, rk_ref,
         gng_ref, gnb_ref, w2_ref, a2_ref, v2_ref, yb_ref, *scratch) = refs
    else:
        (prkv_ref, plora_ref, mu_ref, mul_ref, w0_ref, a0_ref, kk_ref, ka_ref, rk_ref,
         gng_ref, gnb_ref, w2_ref, a2_ref, yb_ref, vfo_ref, *scratch) = refs
    (carry_ref, carryl_ref, s_ref, mask_ref, tmask_ref, ones_ref, tril_ref,
     x_scr, z_scr, bk_scr, v_scr, coef_scr, wall_scr, *late_scr) = scratch
    C = SCAN_CHUNK
    d = yb_ref.shape[1]
    tiles = range(d // TILE)
    sls = [slice(q * TILE, (q + 1) * TILE) for q in tiles]

    @pl.when(pl.program_id(1) == 0)
    def _():
        carry_ref[...] = jnp.zeros_like(carry_ref)
        carryl_ref[...] = jnp.zeros_like(carryl_ref)
        s_ref[...] = jnp.zeros_like(s_ref)

    def iota(shape, dim):
        return lax.broadcasted_iota(jnp.int32, shape, dim)

    def level_mask(t, c, s):
        return (t // (2 * s) == c // (2 * s)) & (t % (2 * s) >= s) & (c % (2 * s) < s)

    def as_bf16(mask):
        return jnp.where(mask, 1.0, 0.0).astype(BF16)

    row_v = iota((C, V7X_LANES), 0)
    lane_v = iota((C, V7X_LANES), 1)
    for hf in range(HEADS_PER_VREG):
        mask_ref[hf] = as_bf16(lane_v // HEAD_DIM == hf)
        for li, s in enumerate(INV_LEVELS):
            mask_ref[2 + 2 * li + hf] = as_bf16(
                (lane_v // HEAD_DIM == hf) & level_mask(row_v, lane_v % HEAD_DIM, s))
    ones_bd = iota((TILE, TILE), 0) // HEAD_DIM == iota((TILE, TILE), 1) // HEAD_DIM
    ones_ref[...] = as_bf16(ones_bd)
    tril_ref[...] = as_bf16(iota((C, C), 0) >= iota((C, C), 1))
    for j in range(TILE // V7X_LANES):
        tmask_ref[j] = as_bf16(iota((TILE, V7X_LANES), 0) // HEAD_DIM
                               == HEADS_PER_VREG * j + iota((TILE, V7X_LANES), 1) // HEAD_DIM)

    tt = iota((C, TILE), 0)
    cc = iota((C, TILE), 1) % HEAD_DIM
    strict = cc < tt
    incl = cc <= tt
    incl2 = jnp.concatenate([incl, incl], axis=1)
    eye = jnp.where(cc == tt, 1.0, 0.0)
    first_row = iota((V7X_SUBLANES, 1), 0) == 0
    zero_v = jnp.zeros((C, V7X_LANES), BF16)

    def bstack(mb, mi):
        blocks = []
        for h in range(HEADS_PER_TILE):
            t, hf = divmod(h, HEADS_PER_VREG)
            piece = mb[:, t * V7X_LANES:(t + 1) * V7X_LANES] * mask_ref[mi + hf]
            blocks.append(jnp.concatenate(
                [piece if i == t else zero_v for i in range(TILE // V7X_LANES)], axis=1))
        return jnp.concatenate(blocks, axis=0)

    def stack_transposed(z):
        zt = jnp.concatenate([z[0:C]] * HEADS_PER_VREG + [z[C:2 * C]] * HEADS_PER_VREG, axis=0).T
        return jnp.concatenate(
            [zt[:, o * V7X_LANES:(o + 1) * V7X_LANES] * tmask_ref[j]
             for o in range(2) for j in range(TILE // V7X_LANES)], axis=1)

    def head_sum(x):
        parts = [p for sl in sls for p in _split_bf16(x[:, sl], 2)]
        s = _dot(jnp.concatenate(parts, axis=0), ones_ref[...])
        return jnp.concatenate(
            [s[2 * q * C:(2 * q + 1) * C] + s[(2 * q + 1) * C:(2 * q + 2) * C] for q in tiles], axis=1)

    def shift_mix(ref, cref, mu, rows, cols):
        p = ref[rows, cols]
        rolled = pltpu.roll(p, 1, 0)
        first = jnp.where(first_row, cref[V7X_SUBLANES - 1:V7X_SUBLANES, cols], rolled[0:V7X_SUBLANES])
        prev = jnp.concatenate([first, rolled[V7X_SUBLANES:]], axis=0)
        cref[:, cols] = p[C - V7X_SUBLANES:C]
        return p + mu[:, cols] * (prev - p)

    def chunk_rows(j):
        return pl.ds(pl.multiple_of(j * C, C), C)

    def prep(j, out):
        rows = chunk_rows(j)
        plm = shift_mix(plora_ref, carryl_ref, mul_ref, rows, slice(None))
        pl_wa = plm[:, 0:V7X_LANES]
        lw = _dot(jnp.tanh(pl_wa).astype(BF16), w2_ref[...])
        la = _dot(pl_wa.astype(BF16), a2_ref[...])
        if has_vfirst:
            lv = _dot(plm[:, V7X_LANES:2 * V7X_LANES].astype(BF16), v2_ref[...])
        yield
        ld = -DECAY_SCALE * jax.nn.sigmoid(w0_ref[...] + lw)
        a = jax.nn.sigmoid(a0_ref[...] + la)
        r = shift_mix(prkv_ref, carry_ref, mu_ref, rows, slice(0, d))
        k = shift_mix(prkv_ref, carry_ref, mu_ref, rows, slice(d, 2 * d))
        v = shift_mix(prkv_ref, carry_ref, mu_ref, rows, slice(2 * d, 3 * d))
        if has_vfirst:
            v = v + (vf_ref[rows, :] - v) * jax.nn.sigmoid(v0_ref[...] + lv)
        kk = k * kk_ref[...]
        n2 = head_sum(kk * kk)
        cl = sum(_dot(tril_ref[...], p) for p in _split_bf16(ld, 3))
        yield
        kk = kk * lax.rsqrt(jnp.maximum(n2, 1e-24))
        k = k * (1.0 + (a - 1.0) * ka_ref[...])
        b = kk * a
        out["coef"] = head_sum(r * k * rk_ref[...])
        yield
        cl_last = cl[C - 1:C, :]
        w_inv = jnp.exp(-cl)
        w_all = jnp.exp(cl_last)
        z = jnp.concatenate([b * w_inv, k * w_inv], axis=0)
        out["x"] = jnp.concatenate([kk * jnp.exp(cl - ld), r * jnp.exp(cl)], axis=0).astype(BF16)
        out["z"] = z.astype(BF16)
        out["bk"] = (z * w_all).astype(BF16)
        out["v"] = v
        out["wall"] = w_all
        yield

    def store_prepared(slot, p):
        x_scr[slot] = p["x"]
        z_scr[slot] = p["z"]
        bk_scr[slot] = p["bk"]
        v_scr[slot] = p["v"]
        coef_scr[slot] = p["coef"]
        wall_scr[slot] = jnp.broadcast_to(p["wall"], wall_scr.shape[1:])

    def load_prepared(slot):
        return {"x": x_scr[slot], "z": z_scr[slot], "bk": bk_scr[slot], "v": v_scr[slot],
                "coef": coef_scr[slot], "wall": wall_scr[slot, 0:1, :]}

    def store_late(p):
        for ref, name in zip(late_scr, ("y", "v", "coef")):
            ref[...] = p[name]

    def load_late():
        return {name: ref[...] for ref, name in zip(late_scr, ("y", "v", "coef"))}

    def phase1(ps):
        jobs = [(p, sl) for p in ps for sl in sls]
        n = range(len(jobs))
        x = [p["x"][:, sl] for p, sl in jobs]
        am = [_dot(x[i], stack_transposed(p["z"][:, sl])) for i, (p, sl) in enumerate(jobs)]
        yield
        nq = [jnp.where(strict, am[i][0:C, 0:TILE], 0.0) for i in n]
        nqb = [m.astype(BF16) for m in nq]
        vb = [p["v"][:, sl].astype(BF16) for p, sl in jobs]
        vstack = [bstack(vb[i], 0) for i in n]
        akm = [jnp.concatenate([jnp.where(strict, am[i][0:C, TILE:2 * TILE], 0.0),
                                jnp.where(incl, am[i][C:2 * C, TILE:2 * TILE], 0.0)], axis=0).astype(BF16)
               for i in n]
        akv2 = [_dot(akm[i], vstack[i]) for i in n]
        akv = [m[0:C] for m in akv2]
        yield
        inv = [eye - jnp.where(level_mask(tt, cc, 1), nq[i], 0.0) for i in n]
        for li in range(len(INV_LEVELS)):
            invb = [m.astype(BF16) for m in inv]
            pe = [_dot(invb[i], bstack(nqb[i], 2 + 2 * li)) for i in n]
            yield
            inv = [inv[i] - _dot(pe[i].astype(BF16), bstack(invb[i], 0)) for i in n]
            yield
        invb = [m.astype(BF16) for m in inv]
        pk = [_dot(invb[i], bstack(x[i][0:C], 0)).astype(BF16) for i in n]
        u0 = [-_dot(invb[i], bstack(akv[i].astype(BF16), 0)) for i in n]
        arb = [jnp.where(incl, am[i][C:2 * C, 0:TILE], 0.0).astype(BF16) for i in n]
        for i, (p, sl) in enumerate(jobs):
            vals = (("pkr", jnp.concatenate([pk[i], x[i][C:2 * C]], axis=0)), ("u0", u0[i]), ("arb", arb[i]),
                    ("arkv", akv2[i][C:2 * C]), ("vb", vb[i]))
            for name, val in vals:
                p.setdefault(name, []).append(val)
        yield

    def phase2(p):
        s0 = [s_ref[q] for q in tiles]
        s0b = [m.astype(BF16) for m in s0]
        gs = [_dot_nt(p["pkr"][q], s0b[q]) for q in tiles]
        ub = [(p["u0"][q] - gs[q][0:C]).astype(BF16) for q in tiles]
        yield
        for q in tiles:
            uv = jnp.concatenate([ub[q], p["vb"][q]], axis=0)
            upd = _dot_tn(uv, p["bk"][:, sls[q]])
            s_ref[q] = s0[q] * p["wall"][:, sls[q]] + jnp.where(ones_bd, upd, 0.0)
        p["y"] = jnp.concatenate(
            [p["arkv"][q] + gs[q][C:2 * C] + _dot(p["arb"][q], bstack(ub[q], 0)) for q in tiles], axis=1)
        yield

    def post(j, p):
        rows = chunk_rows(j)
        y = p["y"]
        if not has_vfirst:
            vfo_ref[rows, :] = p["v"]
        mean = head_sum(y) * (1.0 / HEAD_DIM)
        yield
        yc = y - mean
        var = head_sum(yc * yc) * (1.0 / HEAD_DIM)
        yield
        yn = yc * lax.rsqrt(var + GN_EPS) * gng_ref[...] + gnb_ref[...]
        yb_ref[rows, :] = (yn + p["coef"] * p["v"]).astype(BF16)
        yield

    def pair(i, carry):
        pa, pb = load_prepared(0), load_prepared(1)
        nxt = [{}, {}]

        def chain():
            late = post(jnp.maximum(2 * i - 1, 0), load_late())
            for stage, _ in enumerate(phase1([pa, pb])):
                if stage % 2 == 1:
                    next(late, None)
                yield
            yield from late
            yield from phase2(pa)
            post_a = post(2 * i, pa)
            for _ in phase2(pb):
                next(post_a, None)
                yield
            yield from post_a
            store_late(pb)

        def prepare_next():
            for k in range(2):
                yield from prep(jnp.minimum(2 * i + 2 + k, n_sub - 1), nxt[k])

        _interleave(chain(), prepare_next(), period=2)
        for k in range(2):
            store_prepared(k, nxt[k])
        return carry

    for k in range(2):
        first = {}
        _interleave(prep(k, first))
        store_prepared(k, first)
    for ref in late_scr:
        ref[...] = jnp.zeros_like(ref)
    lax.fori_loop(0, n_sub // 2, pair, 0)
    _interleave(post(n_sub - 1, load_late()))


def _rwkv(prkv, plora, vfirst, mu_rkv, mu_lora, w0, a0, v0, k_k, k_a, r_k, gn_g, gn_b, w2p, a2p, v2p,
          *, batch, seq, tc=1024):
    n, d3 = prkv.shape
    d = d3 // 3
    lw = plora.shape[1]
    nt = seq // tc
    n_sub = tc // SCAN_CHUNK
    assert n_sub % 2 == 0 and HEADS_PER_VREG == 2
    has_vfirst = vfirst is not None
    row = lambda w: pl.BlockSpec((tc, w), lambda b, t: (b * nt + t, 0))
    vec = lambda a: a.reshape(1, -1)
    in_specs = [row(d3), row(lw)]
    args = [prkv, plora]
    if has_vfirst:
        in_specs.append(row(d))
        args.append(vfirst)
    vecs = [mu_rkv, mu_lora, w0, a0] + ([v0] if has_vfirst else []) + [k_k, k_a, r_k, gn_g, gn_b]
    for a in vecs:
        in_specs.append(_const_spec((1, a.size)))
        args.append(vec(a))
    mats = [w2p, a2p] + ([v2p] if has_vfirst else [])
    for m in mats:
        in_specs.append(_const_spec(m.shape))
        args.append(m)
    out_specs = [row(d)]
    out_shape = [jax.ShapeDtypeStruct((n, d), BF16)]
    if not has_vfirst:
        out_specs.append(row(d))
        out_shape.append(jax.ShapeDtypeStruct((n, d), F32))
    C = SCAN_CHUNK
    scratch = [
        pltpu.VMEM((V7X_SUBLANES, d3), F32),
        pltpu.VMEM((V7X_SUBLANES, lw), F32),
        pltpu.VMEM((d // TILE, TILE, TILE), F32),
        pltpu.VMEM((2 + 2 * len(INV_LEVELS), C, V7X_LANES), BF16),
        pltpu.VMEM((TILE // V7X_LANES, TILE, V7X_LANES), BF16),
        pltpu.VMEM((TILE, TILE), BF16),
        pltpu.VMEM((C, C), BF16),
        pltpu.VMEM((2, 2 * C, d), BF16),
        pltpu.VMEM((2, 2 * C, d), BF16),
        pltpu.VMEM((2, 2 * C, d), BF16),
        pltpu.VMEM((2, C, d), F32),
        pltpu.VMEM((2, C, d), F32),
        pltpu.VMEM((2, V7X_SUBLANES, d), F32),
        pltpu.VMEM((C, d), F32),
        pltpu.VMEM((C, d), F32),
        pltpu.VMEM((C, d), F32),
    ]
    outs = pl.pallas_call(
        functools.partial(_rwkv_body, has_vfirst=has_vfirst, n_sub=n_sub),
        grid=(batch, nt),
        in_specs=in_specs,
        out_specs=out_specs,
        out_shape=out_shape,
        scratch_shapes=scratch,
        compiler_params=_params("parallel", "arbitrary"),
        name="rwkv7_first" if not has_vfirst else "rwkv7_rest",
    )(*args)
    if has_vfirst:
        return outs[0], vfirst
    return outs[0], outs[1]


def _merge_body(x_ref, ga_ref, gb_ref, gu_ref, vln_ref, yb_ref, ws_ref, bias_ref, wa_ref, wb_ref, wo_ref,
                o_ref, ya_ref):
    tm, d = x_ref.shape
    for c in range(tm // SGU_CHUNK):
        rows = slice(c * SGU_CHUNK, (c + 1) * SGU_CHUNK)
        for g in range(d // SGU_GROUP_DIM):
            cols = slice(g * SGU_GROUP_DIM, (g + 1) * SGU_GROUP_DIM)
            s = _dot(ws_ref[g], vln_ref[rows, cols]) + bias_ref[:, cols]
            ya_ref[rows, cols] = (gu_ref[rows, cols].astype(F32) * s).astype(BF16)
    pa = _dot(ya_ref[...], wa_ref[...])
    pb = _dot(yb_ref[...], wb_ref[...])
    merged = jax.nn.sigmoid(ga_ref[...]) * pa + jax.nn.sigmoid(gb_ref[...]) * pb
    o_ref[...] = x_ref[...] + _dot(merged.astype(BF16), wo_ref[...])


def _merge(x, ga, gb, gu, vln, yb, ws, bias, wa, wb, wo, *, tm=512):
    n, d = x.shape
    row = pl.BlockSpec((tm, d), lambda i: (i, 0))
    return pl.pallas_call(
        _merge_body,
        grid=(n // tm,),
        in_specs=[row] * 6 + [_const_spec(ws.shape), _const_spec(bias.shape), _const_spec(wa.shape),
                              _const_spec(wb.shape), _const_spec(wo.shape)],
        out_specs=row,
        out_shape=jax.ShapeDtypeStruct((n, d), F32),
        scratch_shapes=[pltpu.VMEM((tm, d), BF16)],
        compiler_params=_params("parallel"),
        name="merge",
    )(x, ga, gb, gu, vln, yb, ws, bias, wa, wb, wo)


def _pad_rows(w, offset, rows):
    out = jnp.zeros((rows, w.shape[1]), BF16)
    return lax.dynamic_update_slice(out, w.astype(BF16), (offset, 0))


def kernel(x, w_in_first, mu_first, w_in_rest, mu_rest, rwkv_v0, rwkv_v2, ffn1_norm, ffn1_w_gu, ffn1_w_down, mix_norm, sgu_ln_g, sgu_ln_b, sgu_w_s, sgu_b_s, rwkv_w0, rwkv_w2, rwkv_a0, rwkv_a2, rwkv_k_k, rwkv_k_a, rwkv_r_k, rwkv_gn_g, rwkv_gn_b, w_proj_a, w_proj_b, w_out, ffn2_norm, ffn2_w_gu, ffn2_w_down, final_norm):
    batch, seq, d = x.shape
    depth = ffn1_norm.shape[0]
    n = batch * seq
    assert d % TILE == 0 and seq % 1024 == 0
    assert sgu_w_s.shape[1:] == (d // SGU_GROUP_DIM, SGU_CHUNK, SGU_CHUNK)
    d_lora = rwkv_w2.shape[1]
    a_lora = rwkv_a2.shape[1]
    lora_w = TILE
    assert d_lora + a_lora == V7X_LANES and rwkv_v2.shape[1] <= V7X_LANES
    causal = jnp.tril(jnp.ones((SGU_CHUNK, SGU_CHUNK), dtype=bool))

    xs = x.reshape(n, d)
    v_first = None
    for layer in range(depth):
        xs = _ffn(xs, ffn1_norm[layer], ffn1_w_gu[layer].astype(BF16), ffn1_w_down[layer].astype(BF16))

        if layer == 0:
            w_in, mu = w_in_first, mu_first
        else:
            w_in, mu = w_in_rest[layer - 1], mu_rest[layer - 1]
        n_low = w_in.shape[1] - 7 * d
        assert n_low <= lora_w
        w_main = w_in[:, :7 * d].astype(BF16)
        w_lora = jnp.pad(w_in[:, 7 * d:], ((0, 0), (0, lora_w - n_low))).astype(BF16)
        mu_rkv = mu[:3 * d]
        mu_lora = jnp.pad(mu[3 * d:], (0, lora_w - n_low))
        ga, gb, gu, vln, prkv, plora = _inproj(xs, mix_norm[layer], w_main, w_lora,
                                               sgu_ln_g[layer], sgu_ln_b[layer])

        w2p = _pad_rows(rwkv_w2[layer], 0, V7X_LANES)
        a2p = _pad_rows(rwkv_a2[layer], d_lora, V7X_LANES)
        if layer == 0:
            v0, v2p = None, None
        else:
            v0 = rwkv_v0[layer - 1]
            v2p = _pad_rows(rwkv_v2[layer - 1], 0, V7X_LANES)
        yb, v_first = _rwkv(prkv, plora, v_first, mu_rkv, mu_lora, rwkv_w0[layer], rwkv_a0[layer], v0,
                            rwkv_k_k[layer], rwkv_k_a[layer], rwkv_r_k[layer].reshape(-1),
                            rwkv_gn_g[layer], rwkv_gn_b[layer], w2p, a2p, v2p, batch=batch, seq=seq)

        ws = jnp.where(causal, sgu_w_s[layer], 0).astype(BF16)
        bias = jnp.repeat(sgu_b_s[layer].T, SGU_GROUP_DIM, axis=1)
        xs = _merge(xs, ga, gb, gu, vln, yb, ws, bias, w_proj_a[layer].astype(BF16),
                    w_proj_b[layer].astype(BF16), w_out[layer].astype(BF16))

        final_g = final_norm if layer == depth - 1 else None
        xs = _ffn(xs, ffn2_norm[layer], ffn2_w_gu[layer].astype(BF16), ffn2_w_down[layer].astype(BF16), final_g)
    return xs.reshape(batch, seq, d)
```

```python
import functools
import math

import jax
import jax.numpy as jnp
from jax import lax
from jax.experimental import pallas as pl
from jax.experimental.pallas import tpu as pltpu

F32 = jnp.float32
BF16 = jnp.bfloat16

NORM_EPS = 1e-6
LN_EPS = 1e-5
GN_EPS = 64e-5

SGU_CHUNK = 128
SGU_GROUP_DIM = 128
HEAD_DIM = 64
DECAY_SCALE = math.exp(-0.5)

V7X_MXU_DIM = 256
V7X_LANES = 128
V7X_SUBLANES = 8
V7X_VMEM_LIMIT_BYTES = 60000 * 1024

SCAN_CHUNK = 64
TILE = V7X_MXU_DIM
HEADS_PER_TILE = TILE // HEAD_DIM
HEADS_PER_VREG = V7X_LANES // HEAD_DIM
INV_LEVELS = tuple(2 ** i for i in range(1, SCAN_CHUNK.bit_length() - 1))


def _params(*semantics):
    return pltpu.CompilerParams(dimension_semantics=semantics, vmem_limit_bytes=V7X_VMEM_LIMIT_BYTES)


def _const_spec(shape):
    return pl.BlockSpec(shape, lambda *_: (0,) * len(shape), pipeline_mode=pl.Buffered(1))


def _rms(x, g):
    return x * lax.rsqrt(jnp.mean(x * x, axis=-1, keepdims=True) + NORM_EPS) * g


def _dot(a, b):
    return jnp.dot(a, b, preferred_element_type=F32)


def _dot_nt(a, b):
    return lax.dot_general(a, b, (((1,), (1,)), ((), ())), preferred_element_type=F32)


def _dot_tn(a, b):
    return lax.dot_general(a, b, (((0,), (0,)), ((), ())), preferred_element_type=F32)


def _ffn_body(*refs, d_ff, f_chunk, final):
    if final:
        x_ref, g_ref, wgu_ref, wd_ref, fg_ref, o_ref = refs
    else:
        x_ref, g_ref, wgu_ref, wd_ref, o_ref = refs
    x = x_ref[...]
    h = _rms(x, g_ref[...]).astype(BF16)
    acc = jnp.zeros_like(x)
    for c in range(d_ff // f_chunk):
        lo = c * f_chunk
        gate = _dot(h, wgu_ref[:, lo:lo + f_chunk])
        up = _dot(h, wgu_ref[:, d_ff + lo:d_ff + lo + f_chunk])
        act = (gate * jax.nn.sigmoid(gate) * up).astype(BF16)
        acc = acc + _dot(act, wd_ref[lo:lo + f_chunk, :])
    y = x + 0.5 * acc
    if final:
        y = _rms(y, fg_ref[...])
    o_ref[...] = y


def _ffn(x, g, wgu, wd, final_g=None, *, tm=1024):
    n, d = x.shape
    d_ff = wd.shape[0]
    final = final_g is not None
    row = pl.BlockSpec((tm, d), lambda i: (i, 0))
    in_specs = [row, _const_spec((1, d)), _const_spec(wgu.shape), _const_spec(wd.shape)]
    args = [x, g.reshape(1, d), wgu, wd]
    if final:
        in_specs.append(_const_spec((1, d)))
        args.append(final_g.reshape(1, d))
    return pl.pallas_call(
        functools.partial(_ffn_body, d_ff=d_ff, f_chunk=V7X_MXU_DIM, final=final),
        grid=(n // tm,),
        in_specs=in_specs,
        out_specs=row,
        out_shape=jax.ShapeDtypeStruct((n, d), F32),
        compiler_params=_params("parallel"),
        name="ffn_final" if final else "ffn",
    )(*args)


def _inproj_body(x_ref, g_ref, w_ref, wl_ref, lng_ref, lnb_ref,
                 ga_ref, gb_ref, gu_ref, vln_ref, prkv_ref, plora_ref):
    d = x_ref.shape[1]
    h = _rms(x_ref[...], g_ref[...]).astype(BF16)
    ga_ref[...] = _dot(h, w_ref[:, 0:d])
    gb_ref[...] = _dot(h, w_ref[:, d:2 * d])
    gu_ref[...] = jax.nn.gelu(_dot(h, w_ref[:, 2 * d:3 * d])).astype(BF16)
    va = jax.nn.gelu(_dot(h, w_ref[:, 3 * d:4 * d]))
    mu = jnp.mean(va, axis=-1, keepdims=True)
    vc = va - mu
    var = jnp.mean(vc * vc, axis=-1, keepdims=True)
    vln_ref[...] = (vc * lax.rsqrt(var + LN_EPS) * lng_ref[...] + lnb_ref[...]).astype(BF16)
    for c in range(3):
        prkv_ref[:, c * d:(c + 1) * d] = _dot(h, w_ref[:, (4 + c) * d:(5 + c) * d])
    plora_ref[...] = _dot(h, wl_ref[...])


def _inproj(x, g, w_main, w_lora, ln_g, ln_b, *, tm=512):
    n, d = x.shape
    lw = w_lora.shape[1]
    row = lambda w: pl.BlockSpec((tm, w), lambda i: (i, 0))
    return pl.pallas_call(
        _inproj_body,
        grid=(n // tm,),
        in_specs=[row(d), _const_spec((1, d)), _const_spec(w_main.shape), _const_spec(w_lora.shape),
                  _const_spec((1, d)), _const_spec((1, d))],
        out_specs=[row(d), row(d), row(d), row(d), row(3 * d), row(lw)],
        out_shape=[jax.ShapeDtypeStruct((n, d), F32), jax.ShapeDtypeStruct((n, d), F32),
                   jax.ShapeDtypeStruct((n, d), BF16), jax.ShapeDtypeStruct((n, d), BF16),
                   jax.ShapeDtypeStruct((n, 3 * d), F32), jax.ShapeDtypeStruct((n, lw), F32)],
        compiler_params=_params("parallel"),
        name="inproj",
    )(x, g.reshape(1, d), w_main, w_lora, ln_g.reshape(1, d), ln_b.reshape(1, d))


def _split_bf16(x, terms):
    parts = []
    for i in range(terms):
        p = x.astype(BF16)
        parts.append(p)
        if i + 1 < terms:
            x = x - p.astype(F32)
    return parts


def _interleave(main, *others, period=3):
    for i, _ in enumerate(main):
        if i % period == 0:
            for o in others:
                next(o, None)
    for o in others:
        for _ in o:
            pass


def _rwkv_body(*refs, has_vfirst, n_sub):
    if has_vfirst:
        (prkv_ref, plora_ref, vf_ref, prkv_nx, plora_nx, vf_nx, mu_ref, mul_ref, w0_ref, a0_ref, v0_ref, kk_ref,
         ka_ref, rk_ref, gng_ref, gnb_ref, w2_ref, a2_ref, v2_ref, yb_ref, *scratch) = refs
    else:
        (prkv_ref, plora_ref, prkv_nx, plora_nx, mu_ref, mul_ref, w0_ref, a0_ref, kk_ref, ka_ref, rk_ref,
         gng_ref, gnb_ref, w2_ref, a2_ref, yb_ref, vfo_ref, *scratch) = refs
        vf_ref = vf_nx = None
    (carry_ref, carryl_ref, s_ref, mask_ref, tmask_ref, ones_ref, tril_ref,
     x_scr, z_scr, bk_scr, v_scr, coef_scr, wall_scr, *late_scr) = scratch
    C = SCAN_CHUNK
    d = yb_ref.shape[1]
    tiles = range(d // TILE)
    sls = [slice(q * TILE, (q + 1) * TILE) for q in tiles]

    def iota(shape, dim):
        return lax.broadcasted_iota(jnp.int32, shape, dim)

    def level_mask(t, c, s):
        return (t // (2 * s) == c // (2 * s)) & (t % (2 * s) >= s) & (c % (2 * s) < s)

    def as_bf16(mask):
        return jnp.where(mask, 1.0, 0.0).astype(BF16)

    row_v = iota((C, V7X_LANES), 0)
    lane_v = iota((C, V7X_LANES), 1)
    for hf in range(HEADS_PER_VREG):
        mask_ref[hf] = as_bf16(lane_v // HEAD_DIM == hf)
        for li, s in enumerate(INV_LEVELS):
            mask_ref[2 + 2 * li + hf] = as_bf16(
                (lane_v // HEAD_DIM == hf) & level_mask(row_v, lane_v % HEAD_DIM, s))
    ones_bd = iota((TILE, TILE), 0) // HEAD_DIM == iota((TILE, TILE), 1) // HEAD_DIM
    ones_ref[...] = as_bf16(ones_bd)
    tril_ref[...] = as_bf16(iota((C, C), 0) >= iota((C, C), 1))
    for j in range(TILE // V7X_LANES):
        tmask_ref[j] = as_bf16(iota((TILE, V7X_LANES), 0) // HEAD_DIM
                               == HEADS_PER_VREG * j + iota((TILE, V7X_LANES), 1) // HEAD_DIM)

    tt = iota((C, TILE), 0)
    cc = iota((C, TILE), 1) % HEAD_DIM
    strict = cc < tt
    incl = cc <= tt
    eye = jnp.where(cc == tt, 1.0, 0.0)
    first_row = iota((V7X_SUBLANES, 1), 0) == 0
    zero_v = jnp.zeros((C, V7X_LANES), BF16)

    def bstack(mb, mi):
        blocks = []
        for h in range(HEADS_PER_TILE):
            t, hf = divmod(h, HEADS_PER_VREG)
            piece = mb[:, t * V7X_LANES:(t + 1) * V7X_LANES] * mask_ref[mi + hf]
            blocks.append(jnp.concatenate(
                [piece if i == t else zero_v for i in range(TILE // V7X_LANES)], axis=1))
        return jnp.concatenate(blocks, axis=0)

    def stack_transposed(z):
        zt = jnp.concatenate([z[0:C]] * HEADS_PER_VREG + [z[C:2 * C]] * HEADS_PER_VREG, axis=0).T
        return jnp.concatenate(
            [zt[:, o * V7X_LANES:(o + 1) * V7X_LANES] * tmask_ref[j]
             for o in range(2) for j in range(TILE // V7X_LANES)], axis=1)

    def head_sum(x):
        s = _dot(jnp.concatenate([x[:, sl].astype(BF16) for sl in sls], axis=0), ones_ref[...])
        return jnp.concatenate([s[q * C:(q + 1) * C] for q in tiles], axis=1)

    def shift_mix(ref, cref, mu, rows, cols):
        p = ref[rows, cols]
        rolled = pltpu.roll(p, 1, 0)
        first = jnp.where(first_row, cref[V7X_SUBLANES - 1:V7X_SUBLANES, cols], rolled[0:V7X_SUBLANES])
        prev = jnp.concatenate([first, rolled[V7X_SUBLANES:]], axis=0)
        cref[:, cols] = p[C - V7X_SUBLANES:C]
        return p + mu[:, cols] * (prev - p)

    def chunk_rows(j):
        return pl.ds(pl.multiple_of(j * C, C), C)

    def prep(src, rows, out):
        p_ref, l_ref, vfirst_ref = src
        plm = shift_mix(l_ref, carryl_ref, mul_ref, rows, slice(None))
        pl_wa = plm[:, 0:V7X_LANES]
        lw = _dot(jnp.tanh(pl_wa).astype(BF16), w2_ref[...])
        la = _dot(pl_wa.astype(BF16), a2_ref[...])
        if has_vfirst:
            lv = _dot(plm[:, V7X_LANES:2 * V7X_LANES].astype(BF16), v2_ref[...])
        yield
        ld = -DECAY_SCALE * jax.nn.sigmoid(w0_ref[...] + lw)
        a = jax.nn.sigmoid(a0_ref[...] + la)
        r = shift_mix(p_ref, carry_ref, mu_ref, rows, slice(0, d))
        k = shift_mix(p_ref, carry_ref, mu_ref, rows, slice(d, 2 * d))
        v = shift_mix(p_ref, carry_ref, mu_ref, rows, slice(2 * d, 3 * d))
        if has_vfirst:
            v = v + (vfirst_ref[rows, :] - v) * jax.nn.sigmoid(v0_ref[...] + lv)
        kk = k * kk_ref[...]
        n2 = head_sum(kk * kk)
        cl = sum(_dot(tril_ref[...], p) for p in _split_bf16(ld, 2))
        yield
        kk = kk * lax.rsqrt(jnp.maximum(n2, 1e-24))
        k = k * (1.0 + (a - 1.0) * ka_ref[...])
        b = kk * a
        out["coef"] = head_sum(r * k * rk_ref[...])
        yield
        cl_last = cl[C - 1:C, :]
        w_inv = jnp.exp(-cl)
        w_all = jnp.exp(cl_last)
        z = jnp.concatenate([b * w_inv, k * w_inv], axis=0)
        out["x"] = jnp.concatenate([kk * jnp.exp(cl - ld), r * jnp.exp(cl)], axis=0).astype(BF16)
        out["z"] = z.astype(BF16)
        out["bk"] = (z * w_all).astype(BF16)
        out["v"] = v
        out["wall"] = w_all
        yield

    def store_prepared(slot, p):
        x_scr[slot] = p["x"]
        z_scr[slot] = p["z"]
        bk_scr[slot] = p["bk"]
        v_scr[slot] = p["v"]
        coef_scr[slot] = p["coef"]
        wall_scr[slot] = jnp.broadcast_to(p["wall"], wall_scr.shape[1:])

    def load_prepared(slot):
        return {"x": x_scr[slot], "z": z_scr[slot], "bk": bk_scr[slot], "v": v_scr[slot],
                "coef": coef_scr[slot], "wall": wall_scr[slot, 0:1, :]}

    def store_late(p):
        for ref, name in zip(late_scr, ("y", "v", "coef")):
            ref[...] = p[name]

    def load_late():
        return {name: ref[...] for ref, name in zip(late_scr, ("y", "v", "coef"))}

    def phase1(ps):
        jobs = [(p, sl) for p in ps for sl in sls]
        n = range(len(jobs))
        x = [p["x"][:, sl] for p, sl in jobs]
        am = [_dot(x[i], stack_transposed(p["z"][:, sl])) for i, (p, sl) in enumerate(jobs)]
        yield
        nq = [jnp.where(strict, am[i][0:C, 0:TILE], 0.0) for i in n]
        nqb = [m.astype(BF16) for m in nq]
        vb = [p["v"][:, sl].astype(BF16) for p, sl in jobs]
        vstack = [bstack(vb[i], 0) for i in n]
        akm = [jnp.concatenate([jnp.where(strict, am[i][0:C, TILE:2 * TILE], 0.0),
                                jnp.where(incl, am[i][C:2 * C, TILE:2 * TILE], 0.0)], axis=0).astype(BF16)
               for i in n]
        akv2 = [_dot(akm[i], vstack[i]) for i in n]
        akv = [m[0:C] for m in akv2]
        yield
        inv = [eye - jnp.where(level_mask(tt, cc, 1), nq[i], 0.0) for i in n]
        for li in range(len(INV_LEVELS)):
            invb = [m.astype(BF16) for m in inv]
            pe = [_dot(invb[i], bstack(nqb[i], 2 + 2 * li)) for i in n]
            yield
            inv = [inv[i] - _dot(pe[i].astype(BF16), bstack(invb[i], 0)) for i in n]
            yield
        invb = [m.astype(BF16) for m in inv]
        pk = [_dot(invb[i], bstack(x[i][0:C], 0)).astype(BF16) for i in n]
        u0 = [-_dot(invb[i], bstack(akv[i].astype(BF16), 0)) for i in n]
        arb = [jnp.where(incl, am[i][C:2 * C, 0:TILE], 0.0).astype(BF16) for i in n]
        for i, (p, sl) in enumerate(jobs):
            vals = (("pkr", jnp.concatenate([pk[i], x[i][C:2 * C]], axis=0)), ("u0", u0[i]), ("arb", arb[i]),
                    ("arkv", akv2[i][C:2 * C]), ("vb", vb[i]))
            for name, val in vals:
                p.setdefault(name, []).append(val)
        yield

    def phase2(p):
        s0 = [s_ref[q] for q in tiles]
        s0b = [m.astype(BF16) for m in s0]
        gs = [_dot_nt(p["pkr"][q], s0b[q]) for q in tiles]
        ub = [(p["u0"][q] - gs[q][0:C]).astype(BF16) for q in tiles]
        yield
        for q in tiles:
            uv = jnp.concatenate([ub[q], p["vb"][q]], axis=0)
            upd = _dot_tn(uv, p["bk"][:, sls[q]])
            s_ref[q] = s0[q] * p["wall"][:, sls[q]] + jnp.where(ones_bd, upd, 0.0)
        p["y"] = jnp.concatenate(
            [p["arkv"][q] + gs[q][C:2 * C] + _dot(p["arb"][q], bstack(ub[q], 0)) for q in tiles], axis=1)
        yield

    def post(j, p):
        rows = chunk_rows(j)
        y = p["y"]
        if not has_vfirst:
            vfo_ref[rows, :] = p["v"]
        mean = head_sum(y) * (1.0 / HEAD_DIM)
        yield
        yc = y - mean
        var = head_sum(yc * yc) * (1.0 / HEAD_DIM)
        yield
        yn = yc * lax.rsqrt(var + GN_EPS) * gng_ref[...] + gnb_ref[...]
        yb_ref[rows, :] = (yn + p["coef"] * p["v"]).astype(BF16)
        yield

    here = (prkv_ref, plora_ref, vf_ref)
    ahead = (prkv_nx, plora_nx, vf_nx)

    def pair(i, prepare_from_next_tile):
        pa, pb = load_prepared(0), load_prepared(1)
        nxt = [{}, {}]

        def chain():
            late = post(jnp.maximum(2 * i - 1, 0), load_late())
            for stage, _ in enumerate(phase1([pa, pb])):
                if stage % 2 == 1:
                    next(late, None)
                yield
            yield from late
            yield from phase2(pa)
            post_a = post(2 * i, pa)
            for _ in phase2(pb):
                next(post_a, None)
                yield
            yield from post_a
            store_late(pb)

        def prepare_next():
            for k in range(2):
                if prepare_from_next_tile:
                    yield from prep(ahead, slice(k * C, (k + 1) * C), nxt[k])
                else:
                    yield from prep(here, chunk_rows(2 * i + 2 + k), nxt[k])

        _interleave(chain(), prepare_next(), period=2)
        for k in range(2):
            store_prepared(k, nxt[k])

    @pl.when(pl.program_id(1) == 0)
    def _():
        s_ref[...] = jnp.zeros_like(s_ref)
        carry_ref[...] = jnp.zeros_like(carry_ref)
        carryl_ref[...] = jnp.zeros_like(carryl_ref)
        for k in range(2):
            first = {}
            _interleave(prep(here, slice(k * C, (k + 1) * C), first))
            store_prepared(k, first)

    for ref in late_scr:
        ref[...] = jnp.zeros_like(ref)

    def body(i, carry):
        pair(i, False)
        return carry

    lax.fori_loop(0, n_sub // 2 - 1, body, 0)
    pair(n_sub // 2 - 1, True)
    _interleave(post(n_sub - 1, load_late()))


def _rwkv(prkv, plora, vfirst, mu_rkv, mu_lora, w0, a0, v0, k_k, k_a, r_k, gn_g, gn_b, w2p, a2p, v2p,
          *, batch, seq, tc=512):
    n, d3 = prkv.shape
    d = d3 // 3
    lw = plora.shape[1]
    nt = seq // tc
    n_sub = tc // SCAN_CHUNK
    C = SCAN_CHUNK
    look = 2 * C
    assert n_sub % 2 == 0 and n_sub >= 4 and HEADS_PER_VREG == 2
    has_vfirst = vfirst is not None
    row = lambda w: pl.BlockSpec((tc, w), lambda b, t: (b * nt + t, 0))
    ahead = lambda w: pl.BlockSpec(
        (look, w), lambda b, t: (jnp.minimum(b * nt + t + 1, batch * nt - 1) * (tc // look), 0))
    vec = lambda a: a.reshape(1, -1)
    streams = [(prkv, d3), (plora, lw)] + ([(vfirst, d)] if has_vfirst else [])
    in_specs = [row(w) for _, w in streams] + [ahead(w) for _, w in streams]
    args = [a for a, _ in streams] * 2
    vecs = [mu_rkv, mu_lora, w0, a0] + ([v0] if has_vfirst else []) + [k_k, k_a, r_k, gn_g, gn_b]
    for a in vecs:
        in_specs.append(_const_spec((1, a.size)))
        args.append(vec(a))
    mats = [w2p, a2p] + ([v2p] if has_vfirst else [])
    for m in mats:
        in_specs.append(_const_spec(m.shape))
        args.append(m)
    out_specs = [row(d)]
    out_shape = [jax.ShapeDtypeStruct((n, d), BF16)]
    if not has_vfirst:
        out_specs.append(row(d))
        out_shape.append(jax.ShapeDtypeStruct((n, d), F32))
    scratch = [
        pltpu.VMEM((V7X_SUBLANES, d3), F32),
        pltpu.VMEM((V7X_SUBLANES, lw), F32),
        pltpu.VMEM((d // TILE, TILE, TILE), F32),
        pltpu.VMEM((2 + 2 * len(INV_LEVELS), C, V7X_LANES), BF16),
        pltpu.VMEM((TILE // V7X_LANES, TILE, V7X_LANES), BF16),
        pltpu.VMEM((TILE, TILE), BF16),
        pltpu.VMEM((C, C), BF16),
        pltpu.VMEM((2, 2 * C, d), BF16),
        pltpu.VMEM((2, 2 * C, d), BF16),
        pltpu.VMEM((2, 2 * C, d), BF16),
        pltpu.VMEM((2, C, d), F32),
        pltpu.VMEM((2, C, d), F32),
        pltpu.VMEM((2, V7X_SUBLANES, d), F32),
        pltpu.VMEM((C, d), F32),
        pltpu.VMEM((C, d), F32),
        pltpu.VMEM((C, d), F32),
    ]
    outs = pl.pallas_call(
        functools.partial(_rwkv_body, has_vfirst=has_vfirst, n_sub=n_sub),
        grid=(batch, nt),
        in_specs=in_specs,
        out_specs=out_specs,
        out_shape=out_shape,
        scratch_shapes=scratch,
        compiler_params=_params("parallel", "arbitrary"),
        name="rwkv7_first" if not has_vfirst else "rwkv7_rest",
    )(*args)
    if has_vfirst:
        return outs[0], vfirst
    return outs[0], outs[1]


def _merge_body(x_ref, ga_ref, gb_ref, gu_ref, vln_ref, yb_ref, ws_ref, bias_ref, wa_ref, wb_ref, wo_ref,
                o_ref, ya_ref):
    tm, d = x_ref.shape
    for c in range(tm // SGU_CHUNK):
        rows = slice(c * SGU_CHUNK, (c + 1) * SGU_CHUNK)
        for g in range(d // SGU_GROUP_DIM):
            cols = slice(g * SGU_GROUP_DIM, (g + 1) * SGU_GROUP_DIM)
            s = _dot(ws_ref[g], vln_ref[rows, cols]) + bias_ref[:, cols]
            ya_ref[rows, cols] = (gu_ref[rows, cols].astype(F32) * s).astype(BF16)
    pa = _dot(ya_ref[...], wa_ref[...])
    pb = _dot(yb_ref[...], wb_ref[...])
    merged = jax.nn.sigmoid(ga_ref[...]) * pa + jax.nn.sigmoid(gb_ref[...]) * pb
    o_ref[...] = x_ref[...] + _dot(merged.astype(BF16), wo_ref[...])


def _merge(x, ga, gb, gu, vln, yb, ws, bias, wa, wb, wo, *, tm=512):
    n, d = x.shape
    row = pl.BlockSpec((tm, d), lambda i: (i, 0))
    return pl.pallas_call(
        _merge_body,
        grid=(n // tm,),
        in_specs=[row] * 6 + [_const_spec(ws.shape), _const_spec(bias.shape), _const_spec(wa.shape),
                              _const_spec(wb.shape), _const_spec(wo.shape)],
        out_specs=row,
        out_shape=jax.ShapeDtypeStruct((n, d), F32),
        scratch_shapes=[pltpu.VMEM((tm, d), BF16)],
        compiler_params=_params("parallel"),
        name="merge",
    )(x, ga, gb, gu, vln, yb, ws, bias, wa, wb, wo)


def _pad_rows(w, offset, rows):
    out = jnp.zeros((rows, w.shape[1]), BF16)
    return lax.dynamic_update_slice(out, w.astype(BF16), (offset, 0))


def kernel(x, w_in_first, mu_first, w_in_rest, mu_rest, rwkv_v0, rwkv_v2, ffn1_norm, ffn1_w_gu, ffn1_w_down, mix_norm, sgu_ln_g, sgu_ln_b, sgu_w_s, sgu_b_s, rwkv_w0, rwkv_w2, rwkv_a0, rwkv_a2, rwkv_k_k, rwkv_k_a, rwkv_r_k, rwkv_gn_g, rwkv_gn_b, w_proj_a, w_proj_b, w_out, ffn2_norm, ffn2_w_gu, ffn2_w_down, final_norm):
    batch, seq, d = x.shape
    depth = ffn1_norm.shape[0]
    n = batch * seq
    assert d % TILE == 0 and seq % 1024 == 0
    assert sgu_w_s.shape[1:] == (d // SGU_GROUP_DIM, SGU_CHUNK, SGU_CHUNK)
    d_lora = rwkv_w2.shape[1]
    a_lora = rwkv_a2.shape[1]
    lora_w = TILE
    assert d_lora + a_lora == V7X_LANES and rwkv_v2.shape[1] <= V7X_LANES
    causal = jnp.tril(jnp.ones((SGU_CHUNK, SGU_CHUNK), dtype=bool))

    xs = x.reshape(n, d)
    v_first = None
    for layer in range(depth):
        xs = _ffn(xs, ffn1_norm[layer], ffn1_w_gu[layer].astype(BF16), ffn1_w_down[layer].astype(BF16))

        if layer == 0:
            w_in, mu = w_in_first, mu_first
        else:
            w_in, mu = w_in_rest[layer - 1], mu_rest[layer - 1]
        n_low = w_in.shape[1] - 7 * d
        assert n_low <= lora_w
        w_main = w_in[:, :7 * d].astype(BF16)
        w_lora = jnp.pad(w_in[:, 7 * d:], ((0, 0), (0, lora_w - n_low))).astype(BF16)
        mu_rkv = mu[:3 * d]
        mu_lora = jnp.pad(mu[3 * d:], (0, lora_w - n_low))
        ga, gb, gu, vln, prkv, plora = _inproj(xs, mix_norm[layer], w_main, w_lora,
                                               sgu_ln_g[layer], sgu_ln_b[layer])

        w2p = _pad_rows(rwkv_w2[layer], 0, V7X_LANES)
        a2p = _pad_rows(rwkv_a2[layer], d_lora, V7X_LANES)
        if layer == 0:
            v0, v2p = None, None
        else:
            v0 = rwkv_v0[layer - 1]
            v2p = _pad_rows(rwkv_v2[layer - 1], 0, V7X_LANES)
        yb, v_first = _rwkv(prkv, plora, v_first, mu_rkv, mu_lora, rwkv_w0[layer], rwkv_a0[layer], v0,
                            rwkv_k_k[layer], rwkv_k_a[layer], rwkv_r_k[layer].reshape(-1),
                            rwkv_gn_g[layer], rwkv_gn_b[layer], w2p, a2p, v2p, batch=batch, seq=seq)

        ws = jnp.where(causal, sgu_w_s[layer], 0).astype(BF16)
        bias = jnp.repeat(sgu_b_s[layer].T, SGU_GROUP_DIM, axis=1)
        xs = _merge(xs, ga, gb, gu, vln, yb, ws, bias, w_proj_a[layer].astype(BF16),
                    w_proj_b[layer].astype(BF16), w_out[layer].astype(BF16))

        final_g = final_norm if layer == depth - 1 else None
        xs = _ffn(xs, ffn2_norm[layer], ffn2_w_gu[layer].astype(BF16), ffn2_w_down[layer].astype(BF16), final_g)
    return xs.reshape(batch, seq, d)
```

```python
import functools
import math

import jax
import jax.numpy as jnp
from jax import lax
from jax.experimental import pallas as pl
from jax.experimental.pallas import tpu as pltpu

F32 = jnp.float32
BF16 = jnp.bfloat16

NORM_EPS = 1e-6
LN_EPS = 1e-5
GN_EPS = 64e-5

SGU_CHUNK = 128
SGU_GROUP_DIM = 128
HEAD_DIM = 64
DECAY_SCALE = math.exp(-0.5)

V7X_MXU_DIM = 256
V7X_LANES = 128
V7X_SUBLANES = 8
V7X_VMEM_LIMIT_BYTES = 60000 * 1024

SCAN_CHUNK = 64
TILE = V7X_MXU_DIM
HEADS_PER_TILE = TILE // HEAD_DIM
HEADS_PER_VREG = V7X_LANES // HEAD_DIM
INV_LEVELS = tuple(2 ** i for i in range(1, SCAN_CHUNK.bit_length() - 1))


def _params(*semantics):
    return pltpu.CompilerParams(dimension_semantics=semantics, vmem_limit_bytes=V7X_VMEM_LIMIT_BYTES)


def _const_spec(shape):
    return pl.BlockSpec(shape, lambda *_: (0,) * len(shape), pipeline_mode=pl.Buffered(1))


def _rms(x, g):
    return x * lax.rsqrt(jnp.mean(x * x, axis=-1, keepdims=True) + NORM_EPS) * g


def _dot(a, b):
    return jnp.dot(a, b, preferred_element_type=F32)


def _dot_nt(a, b):
    return lax.dot_general(a, b, (((1,), (1,)), ((), ())), preferred_element_type=F32)


def _dot_tn(a, b):
    return lax.dot_general(a, b, (((0,), (0,)), ((), ())), preferred_element_type=F32)


def _ffn_body(*refs, d_ff, f_chunk, final):
    if final:
        x_ref, g_ref, wgu_ref, wd_ref, fg_ref, o_ref = refs
    else:
        x_ref, g_ref, wgu_ref, wd_ref, o_ref = refs
    x = x_ref[...]
    h = _rms(x, g_ref[...]).astype(BF16)
    acc = jnp.zeros_like(x)
    for c in range(d_ff // f_chunk):
        lo = c * f_chunk
        gate = _dot(h, wgu_ref[:, lo:lo + f_chunk])
        up = _dot(h, wgu_ref[:, d_ff + lo:d_ff + lo + f_chunk])
        act = (gate * jax.nn.sigmoid(gate) * up).astype(BF16)
        acc = acc + _dot(act, wd_ref[lo:lo + f_chunk, :])
    y = x + 0.5 * acc
    if final:
        y = _rms(y, fg_ref[...])
    o_ref[...] = y


def _ffn(x, g, wgu, wd, final_g=None, *, tm=1024):
    n, d = x.shape
    d_ff = wd.shape[0]
    final = final_g is not None
    row = pl.BlockSpec((tm, d), lambda i: (i, 0))
    in_specs = [row, _const_spec((1, d)), _const_spec(wgu.shape), _const_spec(wd.shape)]
    args = [x, g.reshape(1, d), wgu, wd]
    if final:
        in_specs.append(_const_spec((1, d)))
        args.append(final_g.reshape(1, d))
    return pl.pallas_call(
        functools.partial(_ffn_body, d_ff=d_ff, f_chunk=V7X_MXU_DIM, final=final),
        grid=(n // tm,),
        in_specs=in_specs,
        out_specs=row,
        out_shape=jax.ShapeDtypeStruct((n, d), F32),
        compiler_params=_params("parallel"),
        name="ffn_final" if final else "ffn",
    )(*args)


def _inproj_body(x_ref, g_ref, w_ref, wl_ref, lng_ref, lnb_ref,
                 ga_ref, gb_ref, gu_ref, vln_ref, prkv_ref, plora_ref):
    d = x_ref.shape[1]
    h = _rms(x_ref[...], g_ref[...]).astype(BF16)
    ga_ref[...] = _dot(h, w_ref[:, 0:d])
    gb_ref[...] = _dot(h, w_ref[:, d:2 * d])
    gu_ref[...] = jax.nn.gelu(_dot(h, w_ref[:, 2 * d:3 * d])).astype(BF16)
    va = jax.nn.gelu(_dot(h, w_ref[:, 3 * d:4 * d]))
    mu = jnp.mean(va, axis=-1, keepdims=True)
    vc = va - mu
    var = jnp.mean(vc * vc, axis=-1, keepdims=True)
    vln_ref[...] = (vc * lax.rsqrt(var + LN_EPS) * lng_ref[...] + lnb_ref[...]).astype(BF16)
    for c in range(3):
        prkv_ref[:, c * d:(c + 1) * d] = _dot(h, w_ref[:, (4 + c) * d:(5 + c) * d])
    plora_ref[...] = _dot(h, wl_ref[...])


def _inproj(x, g, w_main, w_lora, ln_g, ln_b, *, tm=512):
    n, d = x.shape
    lw = w_lora.shape[1]
    row = lambda w: pl.BlockSpec((tm, w), lambda i: (i, 0))
    return pl.pallas_call(
        _inproj_body,
        grid=(n // tm,),
        in_specs=[row(d), _const_spec((1, d)), _const_spec(w_main.shape), _const_spec(w_lora.shape),
                  _const_spec((1, d)), _const_spec((1, d))],
        out_specs=[row(d), row(d), row(d), row(d), row(3 * d), row(lw)],
        out_shape=[jax.ShapeDtypeStruct((n, d), F32), jax.ShapeDtypeStruct((n, d), F32),
                   jax.ShapeDtypeStruct((n, d), BF16), jax.ShapeDtypeStruct((n, d), BF16),
                   jax.ShapeDtypeStruct((n, 3 * d), F32), jax.ShapeDtypeStruct((n, lw), F32)],
        compiler_params=_params("parallel"),
        name="inproj",
    )(x, g.reshape(1, d), w_main, w_lora, ln_g.reshape(1, d), ln_b.reshape(1, d))


def _split_bf16(x, terms):
    parts = []
    for i in range(terms):
        p = x.astype(BF16)
        parts.append(p)
        if i + 1 < terms:
            x = x - p.astype(F32)
    return parts


def _interleave(main, *others, period=3):
    for i, _ in enumerate(main):
        if i % period == 0:
            for o in others:
                next(o, None)
    for o in others:
        for _ in o:
            pass


def _rwkv_body(*refs, has_vfirst, n_sub):
    if has_vfirst:
        (prkv_ref, plora_ref, vf_ref, prkv_nx, plora_nx, vf_nx, mu_ref, mul_ref, w0_ref, a0_ref, v0_ref, kk_ref,
         ka_ref, rk_ref, gng_ref, gnb_ref, w2_ref, a2_ref, v2_ref, yb_ref, *scratch) = refs
    else:
        (prkv_ref, plora_ref, prkv_nx, plora_nx, mu_ref, mul_ref, w0_ref, a0_ref, kk_ref, ka_ref, rk_ref,
         gng_ref, gnb_ref, w2_ref, a2_ref, yb_ref, vfo_ref, *scratch) = refs
        vf_ref = vf_nx = None
    (carry_ref, carryl_ref, s_ref, mask_ref, tmask_ref, ones_ref, tril_ref,
     x_scr, z_scr, bk_scr, v_scr, coef_scr, wall_scr, *late_scr) = scratch
    C = SCAN_CHUNK
    d = yb_ref.shape[1]
    tiles = range(d // TILE)
    sls = [slice(q * TILE, (q + 1) * TILE) for q in tiles]

    def iota(shape, dim):
        return lax.broadcasted_iota(jnp.int32, shape, dim)

    def level_mask(t, c, s):
        return (t // (2 * s) == c // (2 * s)) & (t % (2 * s) >= s) & (c % (2 * s) < s)

    def as_bf16(mask):
        return jnp.where(mask, 1.0, 0.0).astype(BF16)

    ones_bd = iota((TILE, TILE), 0) // HEAD_DIM == iota((TILE, TILE), 1) // HEAD_DIM

    def init_constants():
        row_v = iota((C, V7X_LANES), 0)
        lane_v = iota((C, V7X_LANES), 1)
        for hf in range(HEADS_PER_VREG):
            mask_ref[hf] = as_bf16(lane_v // HEAD_DIM == hf)
            for li, s in enumerate(INV_LEVELS):
                mask_ref[2 + 2 * li + hf] = as_bf16(
                    (lane_v // HEAD_DIM == hf) & level_mask(row_v, lane_v % HEAD_DIM, s))
        ones_ref[...] = as_bf16(ones_bd)
        tril_ref[...] = as_bf16(iota((C, C), 0) >= iota((C, C), 1))
        for j in range(TILE // V7X_LANES):
            tmask_ref[j] = as_bf16(iota((TILE, V7X_LANES), 0) // HEAD_DIM
                                   == HEADS_PER_VREG * j + iota((TILE, V7X_LANES), 1) // HEAD_DIM)

    tt = iota((C, TILE), 0)
    cc = iota((C, TILE), 1) % HEAD_DIM
    strict = cc < tt
    incl = cc <= tt
    eye = jnp.where(cc == tt, 1.0, 0.0)
    first_row = iota((V7X_SUBLANES, 1), 0) == 0
    zero_v = jnp.zeros((C, V7X_LANES), BF16)

    def bstack(mb, mi):
        blocks = []
        for h in range(HEADS_PER_TILE):
            t, hf = divmod(h, HEADS_PER_VREG)
            piece = mb[:, t * V7X_LANES:(t + 1) * V7X_LANES] * mask_ref[mi + hf]
            blocks.append(jnp.concatenate(
                [piece if i == t else zero_v for i in range(TILE // V7X_LANES)], axis=1))
        return jnp.concatenate(blocks, axis=0)

    def stack_transposed(z):
        zt = jnp.concatenate([z[0:C]] * HEADS_PER_VREG + [z[C:2 * C]] * HEADS_PER_VREG, axis=0).T
        return jnp.concatenate(
            [zt[:, o * V7X_LANES:(o + 1) * V7X_LANES] * tmask_ref[j]
             for o in range(2) for j in range(TILE // V7X_LANES)], axis=1)

    def head_sum(x):
        s = _dot(jnp.concatenate([x[:, sl].astype(BF16) for sl in sls], axis=0), ones_ref[...])
        return jnp.concatenate([s[q * C:(q + 1) * C] for q in tiles], axis=1)

    def shift_mix(ref, cref, mu, rows, cols):
        p = ref[rows, cols]
        rolled = pltpu.roll(p, 1, 0)
        first = jnp.where(first_row, cref[V7X_SUBLANES - 1:V7X_SUBLANES, cols], rolled[0:V7X_SUBLANES])
        prev = jnp.concatenate([first, rolled[V7X_SUBLANES:]], axis=0)
        cref[:, cols] = p[C - V7X_SUBLANES:C]
        return p + mu[:, cols] * (prev - p)

    def chunk_rows(j):
        return pl.ds(pl.multiple_of(j * C, C), C)

    def prep(src, rows, out):
        p_ref, l_ref, vfirst_ref = src
        plm = shift_mix(l_ref, carryl_ref, mul_ref, rows, slice(None))
        pl_wa = plm[:, 0:V7X_LANES]
        lw = _dot(jnp.tanh(pl_wa).astype(BF16), w2_ref[...])
        la = _dot(pl_wa.astype(BF16), a2_ref[...])
        if has_vfirst:
            lv = _dot(plm[:, V7X_LANES:2 * V7X_LANES].astype(BF16), v2_ref[...])
        yield
        ld = -DECAY_SCALE * jax.nn.sigmoid(w0_ref[...] + lw)
        a = jax.nn.sigmoid(a0_ref[...] + la)
        r = shift_mix(p_ref, carry_ref, mu_ref, rows, slice(0, d))
        k = shift_mix(p_ref, carry_ref, mu_ref, rows, slice(d, 2 * d))
        v = shift_mix(p_ref, carry_ref, mu_ref, rows, slice(2 * d, 3 * d))
        if has_vfirst:
            v = v + (vfirst_ref[rows, :] - v) * jax.nn.sigmoid(v0_ref[...] + lv)
        kk = k * kk_ref[...]
        n2 = head_sum(kk * kk)
        cl = sum(_dot(tril_ref[...], p) for p in _split_bf16(ld, 2))
        yield
        kk = kk * lax.rsqrt(jnp.maximum(n2, 1e-24))
        k = k * (1.0 + (a - 1.0) * ka_ref[...])
        b = kk * a
        out["coef"] = head_sum(r * k * rk_ref[...])
        yield
        cl_last = cl[C - 1:C, :]
        w_inv = jnp.exp(-cl)
        w_all = jnp.exp(cl_last)
        z = jnp.concatenate([b * w_inv, k * w_inv], axis=0)
        out["x"] = jnp.concatenate([kk * jnp.exp(cl - ld), r * jnp.exp(cl)], axis=0).astype(BF16)
        out["z"] = z.astype(BF16)
        out["bk"] = (z * w_all).astype(BF16)
        out["v"] = v
        out["wall"] = w_all
        yield

    def store_prepared(slot, p):
        x_scr[slot] = p["x"]
        z_scr[slot] = p["z"]
        bk_scr[slot] = p["bk"]
        v_scr[slot] = p["v"]
        coef_scr[slot] = p["coef"]
        wall_scr[slot] = jnp.broadcast_to(p["wall"], wall_scr.shape[1:])

    def load_prepared(slot):
        return {"x": x_scr[slot], "z": z_scr[slot], "bk": bk_scr[slot], "v": v_scr[slot],
                "coef": coef_scr[slot], "wall": wall_scr[slot, 0:1, :]}

    def store_late(p):
        for ref, name in zip(late_scr, ("y", "v", "coef")):
            ref[...] = p[name]

    def load_late():
        return {name: ref[...] for ref, name in zip(late_scr, ("y", "v", "coef"))}

    def phase1(ps):
        jobs = [(p, sl) for p in ps for sl in sls]
        n = range(len(jobs))
        x = [p["x"][:, sl] for p, sl in jobs]
        am = [_dot(x[i], stack_transposed(p["z"][:, sl])) for i, (p, sl) in enumerate(jobs)]
        yield
        nq = [jnp.where(strict, am[i][0:C, 0:TILE], 0.0) for i in n]
        nqb = [m.astype(BF16) for m in nq]
        vb = [p["v"][:, sl].astype(BF16) for p, sl in jobs]
        vstack = [bstack(vb[i], 0) for i in n]
        akm = [jnp.concatenate([jnp.where(strict, am[i][0:C, TILE:2 * TILE], 0.0),
                                jnp.where(incl, am[i][C:2 * C, TILE:2 * TILE], 0.0)], axis=0).astype(BF16)
               for i in n]
        akv2 = [_dot(akm[i], vstack[i]) for i in n]
        akv = [m[0:C] for m in akv2]
        yield
        inv = [eye - jnp.where(level_mask(tt, cc, 1), nq[i], 0.0) for i in n]
        for li in range(len(INV_LEVELS)):
            invb = [m.astype(BF16) for m in inv]
            pe = [_dot(invb[i], bstack(nqb[i], 2 + 2 * li)) for i in n]
            yield
            inv = [inv[i] - _dot(pe[i].astype(BF16), bstack(invb[i], 0)) for i in n]
            yield
        invb = [m.astype(BF16) for m in inv]
        pk = [_dot(invb[i], bstack(x[i][0:C], 0)).astype(BF16) for i in n]
        u0 = [-_dot(invb[i], bstack(akv[i].astype(BF16), 0)) for i in n]
        arb = [jnp.where(incl, am[i][C:2 * C, 0:TILE], 0.0).astype(BF16) for i in n]
        for i, (p, sl) in enumerate(jobs):
            vals = (("pkr", jnp.concatenate([pk[i], x[i][C:2 * C]], axis=0)), ("u0", u0[i]), ("arb", arb[i]),
                    ("arkv", akv2[i][C:2 * C]), ("vb", vb[i]))
            for name, val in vals:
                p.setdefault(name, []).append(val)
        yield

    def phase2(p):
        s0 = [s_ref[q] for q in tiles]
        s0b = [m.astype(BF16) for m in s0]
        gs = [_dot_nt(p["pkr"][q], s0b[q]) for q in tiles]
        ub = [(p["u0"][q] - gs[q][0:C]).astype(BF16) for q in tiles]
        yield
        for q in tiles:
            uv = jnp.concatenate([ub[q], p["vb"][q]], axis=0)
            upd = _dot_tn(uv, p["bk"][:, sls[q]])
            s_ref[q] = s0[q] * p["wall"][:, sls[q]] + jnp.where(ones_bd, upd, 0.0)
        p["y"] = jnp.concatenate(
            [p["arkv"][q] + gs[q][C:2 * C] + _dot(p["arb"][q], bstack(ub[q], 0)) for q in tiles], axis=1)
        yield

    def post(j, p):
        rows = chunk_rows(j)
        y = p["y"]
        if not has_vfirst:
            vfo_ref[rows, :] = p["v"]
        mean = head_sum(y) * (1.0 / HEAD_DIM)
        yield
        yc = y - mean
        var = head_sum(yc * yc) * (1.0 / HEAD_DIM)
        yield
        yn = yc * lax.rsqrt(var + GN_EPS) * gng_ref[...] + gnb_ref[...]
        yb_ref[rows, :] = (yn + p["coef"] * p["v"]).astype(BF16)
        yield

    here = (prkv_ref, plora_ref, vf_ref)
    ahead = (prkv_nx, plora_nx, vf_nx)

    def pair(i, prepare_from_next_tile, first=False):
        pa, pb = load_prepared(0), load_prepared(1)
        nxt = [{}, {}]

        def chain():
            late = iter(()) if first else post(2 * i - 1, load_late())
            for stage, _ in enumerate(phase1([pa, pb])):
                if stage % 2 == 1:
                    next(late, None)
                yield
            yield from late
            yield from phase2(pa)
            post_a = post(2 * i, pa)
            for _ in phase2(pb):
                next(post_a, None)
                yield
            yield from post_a
            store_late(pb)

        def prepare_next():
            for k in range(2):
                if prepare_from_next_tile:
                    yield from prep(ahead, slice(k * C, (k + 1) * C), nxt[k])
                else:
                    yield from prep(here, chunk_rows(2 * i + 2 + k), nxt[k])

        _interleave(chain(), prepare_next(), period=2)
        for k in range(2):
            store_prepared(k, nxt[k])

    @pl.when(pl.program_id(1) == 0)
    def _():
        init_constants()
        s_ref[...] = jnp.zeros_like(s_ref)
        carry_ref[...] = jnp.zeros_like(carry_ref)
        carryl_ref[...] = jnp.zeros_like(carryl_ref)
        for k in range(2):
            first = {}
            _interleave(prep(here, slice(k * C, (k + 1) * C), first))
            store_prepared(k, first)

    def body(i, carry):
        pair(i, False)
        return carry

    pair(0, False, first=True)
    lax.fori_loop(1, n_sub // 2 - 1, body, 0)
    pair(n_sub // 2 - 1, True)
    _interleave(post(n_sub - 1, load_late()))


def _rwkv(prkv, plora, vfirst, mu_rkv, mu_lora, w0, a0, v0, k_k, k_a, r_k, gn_g, gn_b, w2p, a2p, v2p,
          *, batch, seq, tc=512):
    n, d3 = prkv.shape
    d = d3 // 3
    lw = plora.shape[1]
    nt = seq // tc
    n_sub = tc // SCAN_CHUNK
    C = SCAN_CHUNK
    look = 2 * C
    assert n_sub % 2 == 0 and n_sub >= 4 and HEADS_PER_VREG == 2
    has_vfirst = vfirst is not None
    row = lambda w: pl.BlockSpec((tc, w), lambda b, t: (b * nt + t, 0))
    ahead = lambda w: pl.BlockSpec(
        (look, w), lambda b, t: (jnp.minimum(b * nt + t + 1, batch * nt - 1) * (tc // look), 0))
    vec = lambda a: a.reshape(1, -1)
    streams = [(prkv, d3), (plora, lw)] + ([(vfirst, d)] if has_vfirst else [])
    in_specs = [row(w) for _, w in streams] + [ahead(w) for _, w in streams]
    args = [a for a, _ in streams] * 2
    vecs = [mu_rkv, mu_lora, w0, a0] + ([v0] if has_vfirst else []) + [k_k, k_a, r_k, gn_g, gn_b]
    for a in vecs:
        in_specs.append(_const_spec((1, a.size)))
        args.append(vec(a))
    mats = [w2p, a2p] + ([v2p] if has_vfirst else [])
    for m in mats:
        in_specs.append(_const_spec(m.shape))
        args.append(m)
    out_specs = [row(d)]
    out_shape = [jax.ShapeDtypeStruct((n, d), BF16)]
    if not has_vfirst:
        out_specs.append(row(d))
        out_shape.append(jax.ShapeDtypeStruct((n, d), F32))
    scratch = [
        pltpu.VMEM((V7X_SUBLANES, d3), F32),
        pltpu.VMEM((V7X_SUBLANES, lw), F32),
        pltpu.VMEM((d // TILE, TILE, TILE), F32),
        pltpu.VMEM((2 + 2 * len(INV_LEVELS), C, V7X_LANES), BF16),
        pltpu.VMEM((TILE // V7X_LANES, TILE, V7X_LANES), BF16),
        pltpu.VMEM((TILE, TILE), BF16),
        pltpu.VMEM((C, C), BF16),
        pltpu.VMEM((2, 2 * C, d), BF16),
        pltpu.VMEM((2, 2 * C, d), BF16),
        pltpu.VMEM((2, 2 * C, d), BF16),
        pltpu.VMEM((2, C, d), F32),
        pltpu.VMEM((2, C, d), F32),
        pltpu.VMEM((2, V7X_SUBLANES, d), F32),
        pltpu.VMEM((C, d), F32),
        pltpu.VMEM((C, d), F32),
        pltpu.VMEM((C, d), F32),
    ]
    outs = pl.pallas_call(
        functools.partial(_rwkv_body, has_vfirst=has_vfirst, n_sub=n_sub),
        grid=(batch, nt),
        in_specs=in_specs,
        out_specs=out_specs,
        out_shape=out_shape,
        scratch_shapes=scratch,
        compiler_params=_params("parallel", "arbitrary"),
        name="rwkv7_first" if not has_vfirst else "rwkv7_rest",
    )(*args)
    if has_vfirst:
        return outs[0], vfirst
    return outs[0], outs[1]


def _merge_body(x_ref, ga_ref, gb_ref, gu_ref, vln_ref, yb_ref, ws_ref, bias_ref, wa_ref, wb_ref, wo_ref,
                o_ref, ya_ref):
    tm, d = x_ref.shape
    for c in range(tm // SGU_CHUNK):
        rows = slice(c * SGU_CHUNK, (c + 1) * SGU_CHUNK)
        for g in range(d // SGU_GROUP_DIM):
            cols = slice(g * SGU_GROUP_DIM, (g + 1) * SGU_GROUP_DIM)
            s = _dot(ws_ref[g], vln_ref[rows, cols]) + bias_ref[:, cols]
            ya_ref[rows, cols] = (gu_ref[rows, cols].astype(F32) * s).astype(BF16)
    pa = _dot(ya_ref[...], wa_ref[...])
    pb = _dot(yb_ref[...], wb_ref[...])
    merged = jax.nn.sigmoid(ga_ref[...]) * pa + jax.nn.sigmoid(gb_ref[...]) * pb
    o_ref[...] = x_ref[...] + _dot(merged.astype(BF16), wo_ref[...])


def _merge(x, ga, gb, gu, vln, yb, ws, bias, wa, wb, wo, *, tm=512):
    n, d = x.shape
    row = pl.BlockSpec((tm, d), lambda i: (i, 0))
    return pl.pallas_call(
        _merge_body,
        grid=(n // tm,),
        in_specs=[row] * 6 + [_const_spec(ws.shape), _const_spec(bias.shape), _const_spec(wa.shape),
                              _const_spec(wb.shape), _const_spec(wo.shape)],
        out_specs=row,
        out_shape=jax.ShapeDtypeStruct((n, d), F32),
        scratch_shapes=[pltpu.VMEM((tm, d), BF16)],
        compiler_params=_params("parallel"),
        name="merge",
    )(x, ga, gb, gu, vln, yb, ws, bias, wa, wb, wo)


def _pad_rows(w, offset, rows):
    out = jnp.zeros((rows, w.shape[1]), BF16)
    return lax.dynamic_update_slice(out, w.astype(BF16), (offset, 0))


def kernel(x, w_in_first, mu_first, w_in_rest, mu_rest, rwkv_v0, rwkv_v2, ffn1_norm, ffn1_w_gu, ffn1_w_down, mix_norm, sgu_ln_g, sgu_ln_b, sgu_w_s, sgu_b_s, rwkv_w0, rwkv_w2, rwkv_a0, rwkv_a2, rwkv_k_k, rwkv_k_a, rwkv_r_k, rwkv_gn_g, rwkv_gn_b, w_proj_a, w_proj_b, w_out, ffn2_norm, ffn2_w_gu, ffn2_w_down, final_norm):
    batch, seq, d = x.shape
    depth = ffn1_norm.shape[0]
    n = batch * seq
    assert d % TILE == 0 and seq % 1024 == 0
    assert sgu_w_s.shape[1:] == (d // SGU_GROUP_DIM, SGU_CHUNK, SGU_CHUNK)
    d_lora = rwkv_w2.shape[1]
    a_lora = rwkv_a2.shape[1]
    lora_w = TILE
    assert d_lora + a_lora == V7X_LANES and rwkv_v2.shape[1] <= V7X_LANES
    causal = jnp.tril(jnp.ones((SGU_CHUNK, SGU_CHUNK), dtype=bool))

    xs = x.reshape(n, d)
    v_first = None
    for layer in range(depth):
        xs = _ffn(xs, ffn1_norm[layer], ffn1_w_gu[layer].astype(BF16), ffn1_w_down[layer].astype(BF16))

        if layer == 0:
            w_in, mu = w_in_first, mu_first
        else:
            w_in, mu = w_in_rest[layer - 1], mu_rest[layer - 1]
        n_low = w_in.shape[1] - 7 * d
        assert n_low <= lora_w
        w_main = w_in[:, :7 * d].astype(BF16)
        w_lora = jnp.pad(w_in[:, 7 * d:], ((0, 0), (0, lora_w - n_low))).astype(BF16)
        mu_rkv = mu[:3 * d]
        mu_lora = jnp.pad(mu[3 * d:], (0, lora_w - n_low))
        ga, gb, gu, vln, prkv, plora = _inproj(xs, mix_norm[layer], w_main, w_lora,
                                               sgu_ln_g[layer], sgu_ln_b[layer])

        w2p = _pad_rows(rwkv_w2[layer], 0, V7X_LANES)
        a2p = _pad_rows(rwkv_a2[layer], d_lora, V7X_LANES)
        if layer == 0:
            v0, v2p = None, None
        else:
            v0 = rwkv_v0[layer - 1]
            v2p = _pad_rows(rwkv_v2[layer - 1], 0, V7X_LANES)
        yb, v_first = _rwkv(prkv, plora, v_first, mu_rkv, mu_lora, rwkv_w0[layer], rwkv_a0[layer], v0,
                            rwkv_k_k[layer], rwkv_k_a[layer], rwkv_r_k[layer].reshape(-1),
                            rwkv_gn_g[layer], rwkv_gn_b[layer], w2p, a2p, v2p, batch=batch, seq=seq)

        ws = jnp.where(causal, sgu_w_s[layer], 0).astype(BF16)
        bias = jnp.repeat(sgu_b_s[layer].T, SGU_GROUP_DIM, axis=1)
        xs = _merge(xs, ga, gb, gu, vln, yb, ws, bias, w_proj_a[layer].astype(BF16),
                    w_proj_b[layer].astype(BF16), w_out[layer].astype(BF16))

        final_g = final_norm if layer == depth - 1 else None
        xs = _ffn(xs, ffn2_norm[layer], ffn2_w_gu[layer].astype(BF16), ffn2_w_down[layer].astype(BF16), final_g)
    return xs.reshape(batch, seq, d)
```

```python
import functools
import math

import jax
import jax.numpy as jnp
from jax import lax
from jax.experimental import pallas as pl
from jax.experimental.pallas import tpu as pltpu

F32 = jnp.float32
BF16 = jnp.bfloat16

NORM_EPS = 1e-6
LN_EPS = 1e-5
GN_EPS = 64e-5

SGU_CHUNK = 128
SGU_GROUP_DIM = 128
HEAD_DIM = 64
DECAY_SCALE = math.exp(-0.5)

V7X_MXU_DIM = 256
V7X_LANES = 128
V7X_SUBLANES = 8
V7X_VMEM_LIMIT_BYTES = 60000 * 1024

SCAN_CHUNK = 64
TILE = V7X_MXU_DIM
HEADS_PER_TILE = TILE // HEAD_DIM
HEADS_PER_VREG = V7X_LANES // HEAD_DIM
INV_LEVELS = tuple(2 ** i for i in range(1, SCAN_CHUNK.bit_length() - 1))


def _params(*semantics):
    return pltpu.CompilerParams(dimension_semantics=semantics, vmem_limit_bytes=V7X_VMEM_LIMIT_BYTES)


def _const_spec(shape):
    return pl.BlockSpec(shape, lambda *_: (0,) * len(shape), pipeline_mode=pl.Buffered(1))


def _rms(x, g):
    return x * lax.rsqrt(jnp.mean(x * x, axis=-1, keepdims=True) + NORM_EPS) * g


def _dot(a, b):
    return jnp.dot(a, b, preferred_element_type=F32)


def _dot_nt(a, b):
    return lax.dot_general(a, b, (((1,), (1,)), ((), ())), preferred_element_type=F32)


def _dot_tn(a, b):
    return lax.dot_general(a, b, (((0,), (0,)), ((), ())), preferred_element_type=F32)


def _ffn_half_step(x, g_ref, wgu_ref, wd_ref, fg_ref, *, d_ff, f_chunk):
    h = _rms(x, g_ref[...]).astype(BF16)
    acc = jnp.zeros_like(x)
    for c in range(d_ff // f_chunk):
        lo = c * f_chunk
        gate = _dot(h, wgu_ref[:, lo:lo + f_chunk])
        up = _dot(h, wgu_ref[:, d_ff + lo:d_ff + lo + f_chunk])
        act = (gate * jax.nn.sigmoid(gate) * up).astype(BF16)
        acc = acc + _dot(act, wd_ref[lo:lo + f_chunk, :])
    y = x + 0.5 * acc
    return y if fg_ref is None else _rms(y, fg_ref[...])


def _ffn_body(*refs, d_ff, f_chunk, final):
    if final:
        x_ref, g_ref, wgu_ref, wd_ref, fg_ref, o_ref = refs
    else:
        x_ref, g_ref, wgu_ref, wd_ref, o_ref = refs
        fg_ref = None
    o_ref[...] = _ffn_half_step(x_ref[...], g_ref, wgu_ref, wd_ref, fg_ref, d_ff=d_ff, f_chunk=f_chunk)


def _ffn(x, g, wgu, wd, final_g=None, *, tm=1024):
    n, d = x.shape
    d_ff = wd.shape[0]
    final = final_g is not None
    row = pl.BlockSpec((tm, d), lambda i: (i, 0))
    in_specs = [row, _const_spec((1, d)), _const_spec(wgu.shape), _const_spec(wd.shape)]
    args = [x, g.reshape(1, d), wgu, wd]
    if final:
        in_specs.append(_const_spec((1, d)))
        args.append(final_g.reshape(1, d))
    return pl.pallas_call(
        functools.partial(_ffn_body, d_ff=d_ff, f_chunk=V7X_MXU_DIM, final=final),
        grid=(n // tm,),
        in_specs=in_specs,
        out_specs=row,
        out_shape=jax.ShapeDtypeStruct((n, d), F32),
        compiler_params=_params("parallel"),
        name="ffn_final" if final else "ffn",
    )(*args)


def _inproj_body(x_ref, g_ref, w_ref, wl_ref, lng_ref, lnb_ref,
                 ga_ref, gb_ref, gu_ref, vln_ref, prkv_ref, plora_ref):
    d = x_ref.shape[1]
    h = _rms(x_ref[...], g_ref[...]).astype(BF16)
    ga_ref[...] = _dot(h, w_ref[:, 0:d])
    gb_ref[...] = _dot(h, w_ref[:, d:2 * d])
    gu_ref[...] = jax.nn.gelu(_dot(h, w_ref[:, 2 * d:3 * d])).astype(BF16)
    va = jax.nn.gelu(_dot(h, w_ref[:, 3 * d:4 * d]))
    mu = jnp.mean(va, axis=-1, keepdims=True)
    vc = va - mu
    var = jnp.mean(vc * vc, axis=-1, keepdims=True)
    vln_ref[...] = (vc * lax.rsqrt(var + LN_EPS) * lng_ref[...] + lnb_ref[...]).astype(BF16)
    for c in range(3):
        prkv_ref[:, c * d:(c + 1) * d] = _dot(h, w_ref[:, (4 + c) * d:(5 + c) * d])
    plora_ref[...] = _dot(h, wl_ref[...])


def _inproj(x, g, w_main, w_lora, ln_g, ln_b, *, tm=512):
    n, d = x.shape
    lw = w_lora.shape[1]
    row = lambda w: pl.BlockSpec((tm, w), lambda i: (i, 0))
    return pl.pallas_call(
        _inproj_body,
        grid=(n // tm,),
        in_specs=[row(d), _const_spec((1, d)), _const_spec(w_main.shape), _const_spec(w_lora.shape),
                  _const_spec((1, d)), _const_spec((1, d))],
        out_specs=[row(d), row(d), row(d), row(d), row(3 * d), row(lw)],
        out_shape=[jax.ShapeDtypeStruct((n, d), F32), jax.ShapeDtypeStruct((n, d), F32),
                   jax.ShapeDtypeStruct((n, d), BF16), jax.ShapeDtypeStruct((n, d), BF16),
                   jax.ShapeDtypeStruct((n, 3 * d), F32), jax.ShapeDtypeStruct((n, lw), F32)],
        compiler_params=_params("parallel"),
        name="inproj",
    )(x, g.reshape(1, d), w_main, w_lora, ln_g.reshape(1, d), ln_b.reshape(1, d))


def _split_bf16(x, terms):
    parts = []
    for i in range(terms):
        p = x.astype(BF16)
        parts.append(p)
        if i + 1 < terms:
            x = x - p.astype(F32)
    return parts


def _interleave(main, *others, period=3):
    for i, _ in enumerate(main):
        if i % period == 0:
            for o in others:
                next(o, None)
    for o in others:
        for _ in o:
            pass


def _rwkv_body(*refs, has_vfirst, n_sub):
    if has_vfirst:
        (prkv_ref, plora_ref, vf_ref, prkv_nx, plora_nx, vf_nx, mu_ref, mul_ref, w0_ref, a0_ref, v0_ref, kk_ref,
         ka_ref, rk_ref, gng_ref, gnb_ref, w2_ref, a2_ref, v2_ref, yb_ref, *scratch) = refs
    else:
        (prkv_ref, plora_ref, prkv_nx, plora_nx, mu_ref, mul_ref, w0_ref, a0_ref, kk_ref, ka_ref, rk_ref,
         gng_ref, gnb_ref, w2_ref, a2_ref, yb_ref, vfo_ref, *scratch) = refs
        vf_ref = vf_nx = None
    (carry_ref, carryl_ref, s_ref, mask_ref, tmask_ref, ones_ref, tril_ref,
     x_scr, z_scr, bk_scr, v_scr, coef_scr, wall_scr, *late_scr) = scratch
    C = SCAN_CHUNK
    d = yb_ref.shape[1]
    tiles = range(d // TILE)
    sls = [slice(q * TILE, (q + 1) * TILE) for q in tiles]

    def iota(shape, dim):
        return lax.broadcasted_iota(jnp.int32, shape, dim)

    def level_mask(t, c, s):
        return (t // (2 * s) == c // (2 * s)) & (t % (2 * s) >= s) & (c % (2 * s) < s)

    def as_bf16(mask):
        return jnp.where(mask, 1.0, 0.0).astype(BF16)

    ones_bd = iota((TILE, TILE), 0) // HEAD_DIM == iota((TILE, TILE), 1) // HEAD_DIM

    def init_constants():
        row_v = iota((C, V7X_LANES), 0)
        lane_v = iota((C, V7X_LANES), 1)
        for hf in range(HEADS_PER_VREG):
            mask_ref[hf] = as_bf16(lane_v // HEAD_DIM == hf)
            for li, s in enumerate(INV_LEVELS):
                mask_ref[2 + 2 * li + hf] = as_bf16(
                    (lane_v // HEAD_DIM == hf) & level_mask(row_v, lane_v % HEAD_DIM, s))
        ones_ref[...] = as_bf16(ones_bd)
        tril_ref[...] = as_bf16(iota((C, C), 0) >= iota((C, C), 1))
        for j in range(TILE // V7X_LANES):
            tmask_ref[j] = as_bf16(iota((TILE, V7X_LANES), 0) // HEAD_DIM
                                   == HEADS_PER_VREG * j + iota((TILE, V7X_LANES), 1) // HEAD_DIM)

    tt = iota((C, TILE), 0)
    cc = iota((C, TILE), 1) % HEAD_DIM
    strict = cc < tt
    incl = cc <= tt
    eye = jnp.where(cc == tt, 1.0, 0.0)
    first_row = iota((V7X_SUBLANES, 1), 0) == 0
    zero_v = jnp.zeros((C, V7X_LANES), BF16)

    def bstack(mb, mi):
        blocks = []
        for h in range(HEADS_PER_TILE):
            t, hf = divmod(h, HEADS_PER_VREG)
            piece = mb[:, t * V7X_LANES:(t + 1) * V7X_LANES] * mask_ref[mi + hf]
            blocks.append(jnp.concatenate(
                [piece if i == t else zero_v for i in range(TILE // V7X_LANES)], axis=1))
        return jnp.concatenate(blocks, axis=0)

    def stack_transposed(z):
        zt = jnp.concatenate([z[0:C]] * HEADS_PER_VREG + [z[C:2 * C]] * HEADS_PER_VREG, axis=0).T
        return jnp.concatenate(
            [zt[:, o * V7X_LANES:(o + 1) * V7X_LANES] * tmask_ref[j]
             for o in range(2) for j in range(TILE // V7X_LANES)], axis=1)

    def head_sum(x):
        s = _dot(jnp.concatenate([x[:, sl].astype(BF16) for sl in sls], axis=0), ones_ref[...])
        return jnp.concatenate([s[q * C:(q + 1) * C] for q in tiles], axis=1)

    def shift_mix(ref, cref, mu, rows, cols):
        p = ref[rows, cols]
        rolled = pltpu.roll(p, 1, 0)
        first = jnp.where(first_row, cref[V7X_SUBLANES - 1:V7X_SUBLANES, cols], rolled[0:V7X_SUBLANES])
        prev = jnp.concatenate([first, rolled[V7X_SUBLANES:]], axis=0)
        cref[:, cols] = p[C - V7X_SUBLANES:C]
        return p + mu[:, cols] * (prev - p)

    def chunk_rows(j):
        return pl.ds(pl.multiple_of(j * C, C), C)

    def prep(src, rows, out):
        p_ref, l_ref, vfirst_ref = src
        plm = shift_mix(l_ref, carryl_ref, mul_ref, rows, slice(None))
        pl_wa = plm[:, 0:V7X_LANES]
        lw = _dot(jnp.tanh(pl_wa).astype(BF16), w2_ref[...])
        la = _dot(pl_wa.astype(BF16), a2_ref[...])
        if has_vfirst:
            lv = _dot(plm[:, V7X_LANES:2 * V7X_LANES].astype(BF16), v2_ref[...])
        yield
        ld = -DECAY_SCALE * jax.nn.sigmoid(w0_ref[...] + lw)
        a = jax.nn.sigmoid(a0_ref[...] + la)
        r = shift_mix(p_ref, carry_ref, mu_ref, rows, slice(0, d))
        k = shift_mix(p_ref, carry_ref, mu_ref, rows, slice(d, 2 * d))
        v = shift_mix(p_ref, carry_ref, mu_ref, rows, slice(2 * d, 3 * d))
        if has_vfirst:
            v = v + (vfirst_ref[rows, :] - v) * jax.nn.sigmoid(v0_ref[...] + lv)
        kk = k * kk_ref[...]
        n2 = head_sum(kk * kk)
        cl = sum(_dot(tril_ref[...], p) for p in _split_bf16(ld, 2))
        yield
        kk = kk * lax.rsqrt(jnp.maximum(n2, 1e-24))
        k = k * (1.0 + (a - 1.0) * ka_ref[...])
        b = kk * a
        out["coef"] = head_sum(r * k * rk_ref[...])
        yield
        cl_last = cl[C - 1:C, :]
        w_inv = jnp.exp(-cl)
        w_all = jnp.exp(cl_last)
        z = jnp.concatenate([b * w_inv, k * w_inv], axis=0)
        out["x"] = jnp.concatenate([kk * jnp.exp(cl - ld), r * jnp.exp(cl)], axis=0).astype(BF16)
        out["z"] = z.astype(BF16)
        out["bk"] = (z * w_all).astype(BF16)
        out["v"] = v
        out["wall"] = w_all
        yield

    def store_prepared(slot, p):
        x_scr[slot] = p["x"]
        z_scr[slot] = p["z"]
        bk_scr[slot] = p["bk"]
        v_scr[slot] = p["v"]
        coef_scr[slot] = p["coef"]
        wall_scr[slot] = jnp.broadcast_to(p["wall"], wall_scr.shape[1:])

    def load_prepared(slot):
        return {"x": x_scr[slot], "z": z_scr[slot], "bk": bk_scr[slot], "v": v_scr[slot],
                "coef": coef_scr[slot], "wall": wall_scr[slot, 0:1, :]}

    def store_late(p):
        for ref, name in zip(late_scr, ("y", "v", "coef")):
            ref[...] = p[name]

    def load_late():
        return {name: ref[...] for ref, name in zip(late_scr, ("y", "v", "coef"))}

    def phase1(ps):
        jobs = [(p, sl) for p in ps for sl in sls]
        n = range(len(jobs))
        x = [p["x"][:, sl] for p, sl in jobs]
        am = [_dot(x[i], stack_transposed(p["z"][:, sl])) for i, (p, sl) in enumerate(jobs)]
        yield
        nq = [jnp.where(strict, am[i][0:C, 0:TILE], 0.0) for i in n]
        nqb = [m.astype(BF16) for m in nq]
        vb = [p["v"][:, sl].astype(BF16) for p, sl in jobs]
        vstack = [bstack(vb[i], 0) for i in n]
        akm = [jnp.concatenate([jnp.where(strict, am[i][0:C, TILE:2 * TILE], 0.0),
                                jnp.where(incl, am[i][C:2 * C, TILE:2 * TILE], 0.0)], axis=0).astype(BF16)
               for i in n]
        akv2 = [_dot(akm[i], vstack[i]) for i in n]
        akv = [m[0:C] for m in akv2]
        yield
        inv = [eye - jnp.where(level_mask(tt, cc, 1), nq[i], 0.0) for i in n]
        for li in range(len(INV_LEVELS)):
            invb = [m.astype(BF16) for m in inv]
            pe = [_dot(invb[i], bstack(nqb[i], 2 + 2 * li)) for i in n]
            yield
            inv = [inv[i] - _dot(pe[i].astype(BF16), bstack(invb[i], 0)) for i in n]
            yield
        invb = [m.astype(BF16) for m in inv]
        pk = [_dot(invb[i], bstack(x[i][0:C], 0)).astype(BF16) for i in n]
        u0 = [-_dot(invb[i], bstack(akv[i].astype(BF16), 0)) for i in n]
        arb = [jnp.where(incl, am[i][C:2 * C, 0:TILE], 0.0).astype(BF16) for i in n]
        for i, (p, sl) in enumerate(jobs):
            vals = (("pkr", jnp.concatenate([pk[i], x[i][C:2 * C]], axis=0)), ("u0", u0[i]), ("arb", arb[i]),
                    ("arkv", akv2[i][C:2 * C]), ("vb", vb[i]))
            for name, val in vals:
                p.setdefault(name, []).append(val)
        yield

    def phase2(p):
        s0 = [s_ref[q] for q in tiles]
        s0b = [m.astype(BF16) for m in s0]
        gs = [_dot_nt(p["pkr"][q], s0b[q]) for q in tiles]
        ub = [(p["u0"][q] - gs[q][0:C]).astype(BF16) for q in tiles]
        yield
        for q in tiles:
            uv = jnp.concatenate([ub[q], p["vb"][q]], axis=0)
            upd = _dot_tn(uv, p["bk"][:, sls[q]])
            s_ref[q] = s0[q] * p["wall"][:, sls[q]] + jnp.where(ones_bd, upd, 0.0)
        p["y"] = jnp.concatenate(
            [p["arkv"][q] + gs[q][C:2 * C] + _dot(p["arb"][q], bstack(ub[q], 0)) for q in tiles], axis=1)
        yield

    def post(j, p):
        rows = chunk_rows(j)
        y = p["y"]
        if not has_vfirst:
            vfo_ref[rows, :] = p["v"]
        mean = head_sum(y) * (1.0 / HEAD_DIM)
        yield
        yc = y - mean
        var = head_sum(yc * yc) * (1.0 / HEAD_DIM)
        yield
        yn = yc * lax.rsqrt(var + GN_EPS) * gng_ref[...] + gnb_ref[...]
        yb_ref[rows, :] = (yn + p["coef"] * p["v"]).astype(BF16)
        yield

    here = (prkv_ref, plora_ref, vf_ref)
    ahead = (prkv_nx, plora_nx, vf_nx)

    def pair(i, prepare_from_next_tile, first=False):
        pa, pb = load_prepared(0), load_prepared(1)
        nxt = [{}, {}]

        def chain():
            late = iter(()) if first else post(2 * i - 1, load_late())
            for stage, _ in enumerate(phase1([pa, pb])):
                if stage % 2 == 1:
                    next(late, None)
                yield
            yield from late
            yield from phase2(pa)
            post_a = post(2 * i, pa)
            for _ in phase2(pb):
                next(post_a, None)
                yield
            yield from post_a
            store_late(pb)

        def prepare_next():
            for k in range(2):
                if prepare_from_next_tile:
                    yield from prep(ahead, slice(k * C, (k + 1) * C), nxt[k])
                else:
                    yield from prep(here, chunk_rows(2 * i + 2 + k), nxt[k])

        _interleave(chain(), prepare_next(), period=2)
        for k in range(2):
            store_prepared(k, nxt[k])

    @pl.when(pl.program_id(1) == 0)
    def _():
        init_constants()
        s_ref[...] = jnp.zeros_like(s_ref)
        carry_ref[...] = jnp.zeros_like(carry_ref)
        carryl_ref[...] = jnp.zeros_like(carryl_ref)
        for k in range(2):
            first = {}
            _interleave(prep(here, slice(k * C, (k + 1) * C), first))
            store_prepared(k, first)

    def body(i, carry):
        pair(i, False)
        return carry

    pair(0, False, first=True)
    lax.fori_loop(1, n_sub // 2 - 1, body, 0)
    pair(n_sub // 2 - 1, True)
    _interleave(post(n_sub - 1, load_late()))


def _rwkv(prkv, plora, vfirst, mu_rkv, mu_lora, w0, a0, v0, k_k, k_a, r_k, gn_g, gn_b, w2p, a2p, v2p,
          *, batch, seq, tc=512):
    n, d3 = prkv.shape
    d = d3 // 3
    lw = plora.shape[1]
    nt = seq // tc
    n_sub = tc // SCAN_CHUNK
    C = SCAN_CHUNK
    look = 2 * C
    assert n_sub % 2 == 0 and n_sub >= 4 and HEADS_PER_VREG == 2
    has_vfirst = vfirst is not None
    row = lambda w: pl.BlockSpec((tc, w), lambda b, t: (b * nt + t, 0))
    ahead = lambda w: pl.BlockSpec(
        (look, w), lambda b, t: (jnp.minimum(b * nt + t + 1, batch * nt - 1) * (tc // look), 0))
    vec = lambda a: a.reshape(1, -1)
    streams = [(prkv, d3), (plora, lw)] + ([(vfirst, d)] if has_vfirst else [])
    in_specs = [row(w) for _, w in streams] + [ahead(w) for _, w in streams]
    args = [a for a, _ in streams] * 2
    vecs = [mu_rkv, mu_lora, w0, a0] + ([v0] if has_vfirst else []) + [k_k, k_a, r_k, gn_g, gn_b]
    for a in vecs:
        in_specs.append(_const_spec((1, a.size)))
        args.append(vec(a))
    mats = [w2p, a2p] + ([v2p] if has_vfirst else [])
    for m in mats:
        in_specs.append(_const_spec(m.shape))
        args.append(m)
    out_specs = [row(d)]
    out_shape = [jax.ShapeDtypeStruct((n, d), BF16)]
    if not has_vfirst:
        out_specs.append(row(d))
        out_shape.append(jax.ShapeDtypeStruct((n, d), F32))
    scratch = [
        pltpu.VMEM((V7X_SUBLANES, d3), F32),
        pltpu.VMEM((V7X_SUBLANES, lw), F32),
        pltpu.VMEM((d // TILE, TILE, TILE), F32),
        pltpu.VMEM((2 + 2 * len(INV_LEVELS), C, V7X_LANES), BF16),
        pltpu.VMEM((TILE // V7X_LANES, TILE, V7X_LANES), BF16),
        pltpu.VMEM((TILE, TILE), BF16),
        pltpu.VMEM((C, C), BF16),
        pltpu.VMEM((2, 2 * C, d), BF16),
        pltpu.VMEM((2, 2 * C, d), BF16),
        pltpu.VMEM((2, 2 * C, d), BF16),
        pltpu.VMEM((2, C, d), F32),
        pltpu.VMEM((2, C, d), F32),
        pltpu.VMEM((2, V7X_SUBLANES, d), F32),
        pltpu.VMEM((C, d), F32),
        pltpu.VMEM((C, d), F32),
        pltpu.VMEM((C, d), F32),
    ]
    outs = pl.pallas_call(
        functools.partial(_rwkv_body, has_vfirst=has_vfirst, n_sub=n_sub),
        grid=(batch, nt),
        in_specs=in_specs,
        out_specs=out_specs,
        out_shape=out_shape,
        scratch_shapes=scratch,
        compiler_params=_params("parallel", "arbitrary"),
        name="rwkv7_first" if not has_vfirst else "rwkv7_rest",
    )(*args)
    if has_vfirst:
        return outs[0], vfirst
    return outs[0], outs[1]


def _merge_body(x_ref, ga_ref, gb_ref, gu_ref, vln_ref, yb_ref, ws_ref, bias_ref, wa_ref, wb_ref, wo_ref,
                o_ref, ya_ref):
    tm, d = x_ref.shape
    for c in range(tm // SGU_CHUNK):
        rows = slice(c * SGU_CHUNK, (c + 1) * SGU_CHUNK)
        for g in range(d // SGU_GROUP_DIM):
            cols = slice(g * SGU_GROUP_DIM, (g + 1) * SGU_GROUP_DIM)
            s = _dot(ws_ref[g], vln_ref[rows, cols]) + bias_ref[:, cols]
            ya_ref[rows, cols] = (gu_ref[rows, cols].astype(F32) * s).astype(BF16)
    pa = _dot(ya_ref[...], wa_ref[...])
    pb = _dot(yb_ref[...], wb_ref[...])
    merged = jax.nn.sigmoid(ga_ref[...]) * pa + jax.nn.sigmoid(gb_ref[...]) * pb
    o_ref[...] = x_ref[...] + _dot(merged.astype(BF16), wo_ref[...])


def _merge(x, ga, gb, gu, vln, yb, ws, bias, wa, wb, wo, *, tm=512):
    n, d = x.shape
    row = pl.BlockSpec((tm, d), lambda i: (i, 0))
    return pl.pallas_call(
        _merge_body,
        grid=(n // tm,),
        in_specs=[row] * 6 + [_const_spec(ws.shape), _const_spec(bias.shape), _const_spec(wa.shape),
                              _const_spec(wb.shape), _const_spec(wo.shape)],
        out_specs=row,
        out_shape=jax.ShapeDtypeStruct((n, d), F32),
        scratch_shapes=[pltpu.VMEM((tm, d), BF16)],
        compiler_params=_params("parallel"),
        name="merge",
    )(x, ga, gb, gu, vln, yb, ws, bias, wa, wb, wo)


def _merge_ffn_body(*refs, d_ff, f_chunk, final):
    (x_ref, ga_ref, gb_ref, gu_ref, vln_ref, yb_ref, ws_ref, bias_ref, wa_ref, wb_ref, wo_ref,
     g_ref, wgu_ref, wd_ref, *rest) = refs
    fg_ref = rest[0] if final else None
    o_ref, ya_ref = rest[-2], rest[-1]
    tm, d = x_ref.shape
    for c in range(tm // SGU_CHUNK):
        rows = slice(c * SGU_CHUNK, (c + 1) * SGU_CHUNK)
        for g in range(d // SGU_GROUP_DIM):
            cols = slice(g * SGU_GROUP_DIM, (g + 1) * SGU_GROUP_DIM)
            s = _dot(ws_ref[g], vln_ref[rows, cols]) + bias_ref[:, cols]
            ya_ref[rows, cols] = (gu_ref[rows, cols].astype(F32) * s).astype(BF16)
    pa = _dot(ya_ref[...], wa_ref[...])
    pb = _dot(yb_ref[...], wb_ref[...])
    merged = jax.nn.sigmoid(ga_ref[...]) * pa + jax.nn.sigmoid(gb_ref[...]) * pb
    x = x_ref[...] + _dot(merged.astype(BF16), wo_ref[...])
    o_ref[...] = _ffn_half_step(x, g_ref, wgu_ref, wd_ref, fg_ref, d_ff=d_ff, f_chunk=f_chunk)


def _merge_ffn(x, ga, gb, gu, vln, yb, ws, bias, wa, wb, wo, g, wgu, wd, final_g=None, *, tm=512):
    n, d = x.shape
    final = final_g is not None
    row = pl.BlockSpec((tm, d), lambda i: (i, 0))
    consts = [ws, bias, wa, wb, wo, g.reshape(1, d), wgu, wd] + ([final_g.reshape(1, d)] if final else [])
    return pl.pallas_call(
        functools.partial(_merge_ffn_body, d_ff=wd.shape[0], f_chunk=V7X_MXU_DIM, final=final),
        grid=(n // tm,),
        in_specs=[row] * 6 + [_const_spec(c.shape) for c in consts],
        out_specs=row,
        out_shape=jax.ShapeDtypeStruct((n, d), F32),
        scratch_shapes=[pltpu.VMEM((tm, d), BF16)],
        compiler_params=_params("parallel"),
        name="merge_ffn_final" if final else "merge_ffn",
    )(x, ga, gb, gu, vln, yb, *consts)


def _pad_rows(w, offset, rows):
    out = jnp.zeros((rows, w.shape[1]), BF16)
    return lax.dynamic_update_slice(out, w.astype(BF16), (offset, 0))


def kernel(x, w_in_first, mu_first, w_in_rest, mu_rest, rwkv_v0, rwkv_v2, ffn1_norm, ffn1_w_gu, ffn1_w_down, mix_norm, sgu_ln_g, sgu_ln_b, sgu_w_s, sgu_b_s, rwkv_w0, rwkv_w2, rwkv_a0, rwkv_a2, rwkv_k_k, rwkv_k_a, rwkv_r_k, rwkv_gn_g, rwkv_gn_b, w_proj_a, w_proj_b, w_out, ffn2_norm, ffn2_w_gu, ffn2_w_down, final_norm):
    batch, seq, d = x.shape
    depth = ffn1_norm.shape[0]
    n = batch * seq
    assert d % TILE == 0 and seq % 1024 == 0
    assert sgu_w_s.shape[1:] == (d // SGU_GROUP_DIM, SGU_CHUNK, SGU_CHUNK)
    d_lora = rwkv_w2.shape[1]
    a_lora = rwkv_a2.shape[1]
    lora_w = TILE
    assert d_lora + a_lora == V7X_LANES and rwkv_v2.shape[1] <= V7X_LANES
    causal = jnp.tril(jnp.ones((SGU_CHUNK, SGU_CHUNK), dtype=bool))

    xs = x.reshape(n, d)
    v_first = None
    for layer in range(depth):
        xs = _ffn(xs, ffn1_norm[layer], ffn1_w_gu[layer].astype(BF16), ffn1_w_down[layer].astype(BF16))

        if layer == 0:
            w_in, mu = w_in_first, mu_first
        else:
            w_in, mu = w_in_rest[layer - 1], mu_rest[layer - 1]
        n_low = w_in.shape[1] - 7 * d
        assert n_low <= lora_w
        w_main = w_in[:, :7 * d].astype(BF16)
        w_lora = jnp.pad(w_in[:, 7 * d:], ((0, 0), (0, lora_w - n_low))).astype(BF16)
        mu_rkv = mu[:3 * d]
        mu_lora = jnp.pad(mu[3 * d:], (0, lora_w - n_low))
        ga, gb, gu, vln, prkv, plora = _inproj(xs, mix_norm[layer], w_main, w_lora,
                                               sgu_ln_g[layer], sgu_ln_b[layer])

        w2p = _pad_rows(rwkv_w2[layer], 0, V7X_LANES)
        a2p = _pad_rows(rwkv_a2[layer], d_lora, V7X_LANES)
        if layer == 0:
            v0, v2p = None, None
        else:
            v0 = rwkv_v0[layer - 1]
            v2p = _pad_rows(rwkv_v2[layer - 1], 0, V7X_LANES)
        yb, v_first = _rwkv(prkv, plora, v_first, mu_rkv, mu_lora, rwkv_w0[layer], rwkv_a0[layer], v0,
                            rwkv_k_k[layer], rwkv_k_a[layer], rwkv_r_k[layer].reshape(-1),
                            rwkv_gn_g[layer], rwkv_gn_b[layer], w2p, a2p, v2p, batch=batch, seq=seq)

        ws = jnp.where(causal, sgu_w_s[layer], 0).astype(BF16)
        bias = jnp.repeat(sgu_b_s[layer].T, SGU_GROUP_DIM, axis=1)
        final_g = final_norm if layer == depth - 1 else None
        xs = _merge_ffn(xs, ga, gb, gu, vln, yb, ws, bias, w_proj_a[layer].astype(BF16),
                        w_proj_b[layer].astype(BF16), w_out[layer].astype(BF16), ffn2_norm[layer],
                        ffn2_w_gu[layer].astype(BF16), ffn2_w_down[layer].astype(BF16), final_g)
    return xs.reshape(batch, seq, d)
```
